```python
import math
import jax, jax.numpy as jnp
from jax import lax
import numpy as np

D_MODEL = 2048
BATCH = 2
SEQ = 8192
DEPTH = 4
DEC_BATCH = 8
DEC_SEQ = 2048
PAST_LEN = 128

N_MIXERS = 2
N_RET_LAYERS = (DEPTH + 1) // 2
N_MLA_LAYERS = DEPTH // 2

ALPHA = (2.0 * DEPTH) ** 0.25
BETA = (8.0 * DEPTH) ** -0.25

RET_HEADS = D_MODEL // 256
RET_QK_DIM = 256
RET_V_DIM = 512
RET_QK_W = RET_HEADS * RET_QK_DIM
RET_V_W = RET_HEADS * RET_V_DIM
RET_IN = 2 * RET_QK_W + 2 * RET_V_W
RET_CHUNK = 128
RET_ROPE_BASE = 10000.0

MLA_HEADS = 16
MLA_Q_RANK = 1536
MLA_KV_RANK = 512
MLA_NOPE = 128
MLA_ROPE = 64
MLA_V = 128
MLA_IN = MLA_Q_RANK + MLA_KV_RANK + MLA_ROPE
MLA_ROPE_BASE = 10000.0
Q_BLOCK = 128

FFN_DIM = 5632
CONV_WIDTH = 3

LN_EPS = 1e-5
RMS_EPS = 1e-6
GN_EPS = 1e-6

kernel_name = "hybrid_retention_mla_convffn_encoder"


def layer_norm(x, g, b):
    xf = x.astype(jnp.float32)
    mu = jnp.mean(xf, axis=-1, keepdims=True)
    var = jnp.mean(jnp.square(xf - mu), axis=-1, keepdims=True)
    y = (xf - mu) * lax.rsqrt(var + LN_EPS) * g.astype(jnp.float32) + b.astype(jnp.float32)
    return y.astype(x.dtype)


def rms_norm(x, g):
    xf = x.astype(jnp.float32)
    y = xf * lax.rsqrt(jnp.mean(jnp.square(xf), axis=-1, keepdims=True) + RMS_EPS)
    return (y * g.astype(jnp.float32)).astype(x.dtype)


def rope_tables(seq, dim, base):
    inv = 1.0 / (base ** (jnp.arange(0, dim // 2, dtype=jnp.float32) * (2.0 / dim)))
    ang = jnp.arange(seq, dtype=jnp.float32)[:, None] * inv[None, :]
    return jnp.cos(ang), jnp.sin(ang)


def apply_rope(x, cos, sin):
    half = x.shape[-1] // 2
    x1 = x[..., :half].astype(jnp.float32)
    x2 = x[..., half:].astype(jnp.float32)
    y = jnp.concatenate([x1 * cos - x2 * sin, x1 * sin + x2 * cos], axis=-1)
    return y.astype(x.dtype)


def retention_scan(q, k, v, log_g, strict):
    B, H, S, dk = q.shape
    dv = v.shape[-1]
    n_chunks = S // RET_CHUNK
    idx = jnp.arange(RET_CHUNK, dtype=jnp.float32)
    rel = idx[:, None] - idx[None, :]
    mask = (rel > 0) if strict else (rel >= 0)
    d_intra = jnp.where(mask[None], jnp.exp(log_g[:, None, None] * jnp.maximum(rel, 0.0)[None]), 0.0)
    xi = jnp.exp(log_g[:, None] * (idx[None, :] + 1.0))[:, :, None]
    zeta = jnp.exp(log_g[:, None] * (RET_CHUNK - 1.0 - idx[None, :]))[:, :, None]
    g_chunk = jnp.exp(log_g * RET_CHUNK)[:, None, None]

    def to_chunks(t):
        return t.reshape(B, H, n_chunks, RET_CHUNK, t.shape[-1]).transpose(2, 0, 1, 3, 4)

    def step(state, inp):
        qc, kc, vc = inp
        s = jnp.einsum('bhid,bhjd->bhij', qc, kc) * d_intra
        o = jnp.einsum('bhij,bhjv->bhiv', s, vc) + jnp.einsum('bhid,bhdv->bhiv', qc * xi, state)
        state = g_chunk * state + jnp.einsum('bhjd,bhjv->bhdv', kc * zeta, vc)
        return state, o

    state0 = jnp.zeros((B, H, dk, dv), jnp.float32)
    _, o = lax.scan(step, state0, (to_chunks(q), to_chunks(k), to_chunks(v)))
    return o.transpose(1, 2, 0, 3, 4).reshape(B, H, S, dv)


def retention_mixer(x, w_in, decay_fwd, decay_bwd, w_out):
    B, S, _ = x.shape
    p = x @ w_in
    q = p[..., :RET_QK_W].reshape(B, S, RET_HEADS, RET_QK_DIM)
    k = p[..., RET_QK_W:2 * RET_QK_W].reshape(B, S, RET_HEADS, RET_QK_DIM)
    v = p[..., 2 * RET_QK_W:2 * RET_QK_W + RET_V_W].reshape(B, S, RET_HEADS, RET_V_DIM)
    gate = p[..., 2 * RET_QK_W + RET_V_W:]
    cos, sin = rope_tables(S, RET_QK_DIM, RET_ROPE_BASE)
    q = apply_rope(q, cos[:, None], sin[:, None])
    k = apply_rope(k, cos[:, None], sin[:, None])
    qf = q.astype(jnp.float32).transpose(0, 2, 1, 3)
    kf = k.astype(jnp.float32).transpose(0, 2, 1, 3) * (RET_QK_DIM ** -0.5)
    vf = v.astype(jnp.float32).transpose(0, 2, 1, 3)
    log_f = jax.nn.log_sigmoid(decay_fwd.astype(jnp.float32))
    log_b = jax.nn.log_sigmoid(decay_bwd.astype(jnp.float32))
    o_fwd = retention_scan(qf, kf, vf, log_f, False)
    o_bwd = jnp.flip(retention_scan(jnp.flip(qf, 2), jnp.flip(kf, 2), jnp.flip(vf, 2), log_b, True), 2)
    o = o_fwd + o_bwd
    mu = jnp.mean(o, axis=-1, keepdims=True)
    var = jnp.mean(jnp.square(o - mu), axis=-1, keepdims=True)
    o = (o - mu) * lax.rsqrt(var + GN_EPS)
    o = o.transpose(0, 2, 1, 3).reshape(B, S, RET_V_W).astype(x.dtype)
    return (jax.nn.silu(gate) * o) @ w_out


def mla_mixer(x, w_in, q_norm, kv_norm, w_uq, w_ukv, w_out):
    B, S, _ = x.shape
    c = x @ w_in
    cq = rms_norm(c[..., :MLA_Q_RANK], q_norm)
    ckv = rms_norm(c[..., MLA_Q_RANK:MLA_Q_RANK + MLA_KV_RANK], kv_norm)
    k_rope = c[..., MLA_Q_RANK + MLA_KV_RANK:]
    q = (cq @ w_uq).reshape(B, S, MLA_HEADS, MLA_NOPE + MLA_ROPE)
    kv = (ckv @ w_ukv).reshape(B, S, MLA_HEADS, MLA_NOPE + MLA_V)
    q_nope, q_rope = q[..., :MLA_NOPE], q[..., MLA_NOPE:]
    k_nope, v = kv[..., :MLA_NOPE], kv[..., MLA_NOPE:]
    cos, sin = rope_tables(S, MLA_ROPE, MLA_ROPE_BASE)
    q_rope = apply_rope(q_rope, cos[:, None], sin[:, None])
    k_rope = apply_rope(k_rope, cos, sin)
    scale = (MLA_NOPE + MLA_ROPE) ** -0.5
    n_blocks = S // Q_BLOCK
    qn_blocks = q_nope.reshape(B, n_blocks, Q_BLOCK, MLA_HEADS, MLA_NOPE).transpose(1, 0, 2, 3, 4)
    qr_blocks = q_rope.reshape(B, n_blocks, Q_BLOCK, MLA_HEADS, MLA_ROPE).transpose(1, 0, 2, 3, 4)

    def attend(args):
        qn, qr = args
        s = jnp.einsum('bqhd,bkhd->bhqk', qn, k_nope) + jnp.einsum('bqhr,bkr->bhqk', qr, k_rope)
        pr = jax.nn.softmax(s.astype(jnp.float32) * scale, axis=-1).astype(v.dtype)
        return jnp.einsum('bhqk,bkhd->bqhd', pr, v)

    o = lax.map(attend, (qn_blocks, qr_blocks))
    o = o.transpose(1, 0, 2, 3, 4).reshape(B, S, MLA_HEADS * MLA_V)
    return o @ w_out


def conv_ffn(x, w_up, conv_w, conv_b, w_down):
    h = x @ w_up
    u, g = h[..., :FFN_DIM], h[..., FFN_DIM:]
    gp = jnp.pad(g, ((0, 0), (1, 1), (0, 0)))
    g = gp[:, :-2] * conv_w[0] + gp[:, 1:-1] * conv_w[1] + gp[:, 2:] * conv_w[2] + conv_b
    return (jax.nn.silu(g) * u) @ w_down


def trunk(x, ret_w_in, ret_decay_fwd, ret_decay_bwd, ret_w_out,
          mla_w_in, mla_q_norm, mla_kv_norm, mla_w_uq, mla_w_ukv, mla_w_out,
          ln1_g, ln1_b, ln2_g, ln2_b, ffn_w_up, ffn_conv_w, ffn_conv_b, ffn_w_down):
    for i in range(DEPTH):
        j = i // N_MIXERS
        if i % N_MIXERS == 0:
            m = retention_mixer(x, ret_w_in[j], ret_decay_fwd[j], ret_decay_bwd[j], ret_w_out[j])
        else:
            m = mla_mixer(x, mla_w_in[j], mla_q_norm[j], mla_kv_norm[j],
                          mla_w_uq[j], mla_w_ukv[j], mla_w_out[j])
        x = layer_norm(ALPHA * x + m, ln1_g[i], ln1_b[i])
        f = conv_ffn(x, ffn_w_up[i], ffn_conv_w[i], ffn_conv_b[i], ffn_w_down[i])
        x = layer_norm(ALPHA * x + f, ln2_g[i], ln2_b[i])
    return x


def _dense(k, shape, fan_in, scale=1.0):
    return jax.random.normal(k, shape, jnp.float32) * (scale * fan_in ** -0.5)


def setup_inputs(seed: int = 0) -> dict:
    key = jax.random.key(seed)
    ks = jax.random.split(key, 20)
    expo = 5.0 + jnp.arange(RET_HEADS, dtype=jnp.float32)
    decay_logit = jnp.log(jnp.exp2(expo) - 1.0)
    nrm = lambda k, shape: jax.random.normal(k, shape, jnp.float32)
    return {
        "x_prompt": nrm(ks[0], (BATCH, SEQ, D_MODEL)),
        "x_sample": nrm(ks[1], (DEC_BATCH, DEC_SEQ, D_MODEL)),
        "ret_w_in": _dense(ks[2], (N_RET_LAYERS, D_MODEL, RET_IN), D_MODEL),
        "ret_decay_fwd": decay_logit + 0.05 * nrm(ks[3], (N_RET_LAYERS, RET_HEADS)),
        "ret_decay_bwd": decay_logit + 0.05 * nrm(ks[4], (N_RET_LAYERS, RET_HEADS)),
        "ret_w_out": _dense(ks[5], (N_RET_LAYERS, RET_V_W, D_MODEL), RET_V_W, BETA),
        "mla_w_in": _dense(ks[6], (N_MLA_LAYERS, D_MODEL, MLA_IN), D_MODEL),
        "mla_q_norm": 1.0 + 0.02 * nrm(ks[7], (N_MLA_LAYERS, MLA_Q_RANK)),
        "mla_kv_norm": 1.0 + 0.02 * nrm(ks[8], (N_MLA_LAYERS, MLA_KV_RANK)),
        "mla_w_uq": _dense(ks[9], (N_MLA_LAYERS, MLA_Q_RANK, MLA_HEADS * (MLA_NOPE + MLA_ROPE)), MLA_Q_RANK),
        "mla_w_ukv": _dense(ks[10], (N_MLA_LAYERS, MLA_KV_RANK, MLA_HEADS * (MLA_NOPE + MLA_V)), MLA_KV_RANK),
        "mla_w_out": _dense(ks[11], (N_MLA_LAYERS, MLA_HEADS * MLA_V, D_MODEL), MLA_HEADS * MLA_V, BETA),
        "ln1_g": 1.0 + 0.02 * nrm(ks[12], (DEPTH, D_MODEL)),
        "ln1_b": 0.02 * nrm(ks[13], (DEPTH, D_MODEL)),
        "ln2_g": 1.0 + 0.02 * nrm(ks[14], (DEPTH, D_MODEL)),
        "ln2_b": 0.02 * nrm(ks[15], (DEPTH, D_MODEL)),
        "ffn_w_up": _dense(ks[16], (DEPTH, D_MODEL, 2 * FFN_DIM), D_MODEL),
        "ffn_conv_w": _dense(ks[17], (DEPTH, CONV_WIDTH, FFN_DIM), CONV_WIDTH),
        "ffn_conv_b": 0.02 * nrm(ks[18], (DEPTH, FFN_DIM)),
        "ffn_w_down": _dense(ks[19], (DEPTH, FFN_DIM, D_MODEL), FFN_DIM, BETA),
    }


def reference(x_prompt, x_sample, ret_w_in, ret_decay_fwd, ret_decay_bwd, ret_w_out,
              mla_w_in, mla_q_norm, mla_kv_norm, mla_w_uq, mla_w_ukv, mla_w_out,
              ln1_g, ln1_b, ln2_g, ln2_b, ffn_w_up, ffn_conv_w, ffn_conv_b, ffn_w_down):
    weights = (ret_w_in, ret_decay_fwd, ret_decay_bwd, ret_w_out,
               mla_w_in, mla_q_norm, mla_kv_norm, mla_w_uq, mla_w_ukv, mla_w_out,
               ln1_g, ln1_b, ln2_g, ln2_b, ffn_w_up, ffn_conv_w, ffn_conv_b, ffn_w_down)
    y_prompt = trunk(x_prompt, *weights)
    y_sample = trunk(x_sample, *weights)
    return (y_prompt, y_sample)
```

```python
import functools
import math

import jax
import jax.numpy as jnp
import numpy as np
from jax import lax
from jax.experimental import pallas as pl
from jax.experimental.pallas import tpu as pltpu

F32 = jnp.float32
BF16 = jnp.bfloat16

D_MODEL = 2048
DEPTH = 4
ALPHA = (2.0 * DEPTH) ** 0.25

RET_HEADS = 8
RET_QK_DIM = 256
RET_V_DIM = 512
RET_QK_W = RET_HEADS * RET_QK_DIM
RET_V_W = RET_HEADS * RET_V_DIM
RET_IN = 2 * RET_QK_W + 2 * RET_V_W
RET_ROPE_BASE = 10000.0

MLA_HEADS = 16
MLA_Q_RANK = 1536
MLA_KV_RANK = 512
MLA_NOPE = 128
MLA_ROPE = 64
MLA_V = 128
MLA_ROPE_BASE = 10000.0
MLA_HEAD_W = 256
MLA_IN_W = MLA_Q_RANK + MLA_KV_RANK + 128

FFN_DIM = 5632

LN_EPS = 1e-5
RMS_EPS = 1e-6
GN_EPS = 1e-6

VMEM_LIMIT_BYTES = 56 * 1024 * 1024
LOG2_E = 1.4426950408889634


def _params(semantics):
    return pltpu.CompilerParams(dimension_semantics=semantics,
                                vmem_limit_bytes=VMEM_LIMIT_BYTES)


def _dot(a, b):
    return jnp.dot(a, b, preferred_element_type=F32)


def _dot_nt(a, b):
    return lax.dot_general(a, b, (((1,), (1,)), ((), ())), preferred_element_type=F32)


def _dot_tn(a, b):
    return lax.dot_general(a, b, (((0,), (0,)), ((), ())), preferred_element_type=F32)


def _ret_in_kernel(x_ref, w_ref, cos_ref, sin_ref, o_ref, *, bn):
    j = pl.program_id(1)
    acc = _dot(x_ref[...], w_ref[...])
    n_qk_tiles = 2 * RET_QK_W // bn

    @pl.when(j < n_qk_tiles)
    def _():
        cos = cos_ref[...]
        sin = sin_ref[...]
        scale = jnp.where(j >= n_qk_tiles // 2, RET_QK_DIM ** -0.5, 1.0).astype(F32)
        half = RET_QK_DIM // 2
        for h in range(bn // RET_QK_DIM):
            lo = h * RET_QK_DIM
            x1 = acc[:, lo:lo + half]
            x2 = acc[:, lo + half:lo + 2 * half]
            o_ref[:, lo:lo + half] = ((x1 * cos - x2 * sin) * scale).astype(BF16)
            o_ref[:, lo + half:lo + 2 * half] = ((x1 * sin + x2 * cos) * scale).astype(BF16)

    @pl.when(j >= n_qk_tiles)
    def _():
        o_ref[...] = acc.astype(BF16)


def _ret_in_proj(x_bf, w_bf, cos, sin, seq):
    m = x_bf.shape[0]
    bm = min(1024, seq)
    bn = 1024
    n_pos_blocks = seq // bm
    return pl.pallas_call(
        functools.partial(_ret_in_kernel, bn=bn),
        grid=(m // bm, RET_IN // bn),
        in_specs=[
            pl.BlockSpec((bm, D_MODEL), lambda i, j: (i, 0)),
            pl.BlockSpec((D_MODEL, bn), lambda i, j: (0, j)),
            pl.BlockSpec((bm, RET_QK_DIM // 2), lambda i, j: (i % n_pos_blocks, 0)),
            pl.BlockSpec((bm, RET_QK_DIM // 2), lambda i, j: (i % n_pos_blocks, 0)),
        ],
        out_specs=pl.BlockSpec((bm, bn), lambda i, j: (i, j)),
        out_shape=jax.ShapeDtypeStruct((m, RET_IN), BF16),
        compiler_params=_params(("parallel", "arbitrary")),
        name="ret_in_proj",
    )(x_bf, w_bf, cos, sin)


def _log_sigmoid(x):
    return jnp.minimum(x, 0.0) - jnp.log1p(jnp.exp(-jnp.abs(x)))


def _retention_kernel(q_ref, k_ref, v_ref, g_ref, df_ref, db_ref, o_ref,
                      state_ref, obuf_ref, dmat_ref, xi_ref, zeta_ref, *, chunk, n_sub):
    ps = pl.program_id(2)
    t = pl.program_id(3)
    nt = pl.num_programs(3)
    c = chunk

    @pl.when(t == 0)
    def _():
        state_ref[...] = jnp.zeros_like(state_ref)

    @pl.when((t == 0) & (ps == 0))
    def _():
        lf = _log_sigmoid(df_ref[0])
        lb = _log_sigmoid(db_ref[0])
        row = lax.broadcasted_iota(jnp.int32, (c, 128), 0).astype(F32)
        xi_ref[0] = jnp.exp(lb * (c - row))
        xi_ref[1] = jnp.exp(lf * (row + 1.0))
        zeta_ref[0] = jnp.exp(lb * row)
        zeta_ref[1] = jnp.exp(lf * (c - 1.0 - row))
        for jb in range(c // 128):
            col = lax.broadcasted_iota(jnp.int32, (c, 128), 1).astype(F32) + (128.0 * jb)
            rel = row - col
            dmat_ref[:, jb * 128:(jb + 1) * 128] = jnp.where(
                rel >= 0.0, jnp.exp(lf * jnp.maximum(rel, 0.0)), jnp.exp(lb * jnp.maximum(-rel, 0.0)))

    def tile128(tab, width):
        return jnp.concatenate([tab] * (width // 128), axis=1)

    def state_step(qc, kc, vc, direction):
        xi = xi_ref[direction]
        zeta = zeta_ref[direction]
        st = state_ref[...]
        inter = _dot(qc, st.astype(BF16)) * tile128(xi, RET_V_DIM)
        kz = (kc.astype(F32) * tile128(zeta, RET_QK_DIM)).astype(BF16)
        g_chunk = xi[c - 1:c, :] if direction == 1 else xi[0:1, :]
        state_ref[...] = st * tile128(g_chunk, RET_V_DIM) + _dot_tn(kz, vc)
        return inter

    @pl.when(ps == 0)
    def _():
        for s in reversed(range(n_sub)):
            rows = pl.ds(s * c, c)
            inter = state_step(q_ref[0, rows, :], k_ref[0, rows, :], v_ref[0, rows, :], 0)
            base = pl.multiple_of(((nt - 1 - t) * n_sub + s) * c, c)
            obuf_ref[pl.ds(base, c), :] = inter

    @pl.when(ps == 1)
    def _():
        for s in range(n_sub):
            rows = pl.ds(s * c, c)
            qc = q_ref[0, rows, :]
            kc = k_ref[0, rows, :]
            vc = v_ref[0, rows, :]
            sc = (_dot_nt(qc, kc) * dmat_ref[...]).astype(BF16)
            base = pl.multiple_of((t * n_sub + s) * c, c)
            o = _dot(sc, vc) + obuf_ref[pl.ds(base, c), :]
            o = o + state_step(qc, kc, vc, 1)
            mu = jnp.mean(o, axis=-1, keepdims=True)
            d = o - mu
            var = jnp.mean(d * d, axis=-1, keepdims=True)
            o = d * lax.rsqrt(var + GN_EPS)
            gate = g_ref[0, rows, :].astype(F32)
            o_ref[0, rows, :] = (gate * jax.nn.sigmoid(gate) * o).astype(BF16)


def _retention(p, decay_f, decay_b, batch, seq):
    chunk = min(128, seq)
    tb = min(512, seq)
    n_sub = tb // chunk
    nt = seq // tb
    p3 = p.reshape(batch, seq, RET_IN)
    qk_blocks = RET_QK_W // RET_QK_DIM
    v_first = 2 * RET_QK_W // RET_V_DIM
    g_first = v_first + RET_HEADS

    def pos(ps, t):
        return jnp.where(ps == 0, nt - 1 - t, t)

    return pl.pallas_call(
        functools.partial(_retention_kernel, chunk=chunk, n_sub=n_sub),
        grid=(batch, RET_HEADS, 2, nt),
        in_specs=[
            pl.BlockSpec((1, tb, RET_QK_DIM), lambda b, h, ps, t: (b, pos(ps, t), h)),
            pl.BlockSpec((1, tb, RET_QK_DIM), lambda b, h, ps, t: (b, pos(ps, t), qk_blocks + h)),
            pl.BlockSpec((1, tb, RET_V_DIM), lambda b, h, ps, t: (b, pos(ps, t), v_first + h)),
            pl.BlockSpec((1, tb, RET_V_DIM), lambda b, h, ps, t: (b, t * ps, g_first + h)),
            pl.BlockSpec((1, 1, 128), lambda b, h, ps, t: (h, 0, 0)),
            pl.BlockSpec((1, 1, 128), lambda b, h, ps, t: (h, 0, 0)),
        ],
        out_specs=pl.BlockSpec((1, tb, RET_V_DIM), lambda b, h, ps, t: (b, t * ps, h)),
        out_shape=jax.ShapeDtypeStruct((batch, seq, RET_V_W), BF16),
        scratch_shapes=[
            pltpu.VMEM((RET_QK_DIM, RET_V_DIM), F32),
            pltpu.VMEM((seq, RET_V_DIM), F32),
            pltpu.VMEM((chunk, chunk), F32),
            pltpu.VMEM((2, chunk, 128), F32),
            pltpu.VMEM((2, chunk, 128), F32),
        ],
        compiler_params=_params(("parallel", "parallel", "arbitrary", "arbitrary")),
        name="retention",
    )(p3, p3, p3, p3, decay_f, decay_b).reshape(batch * seq, RET_V_W)


def _mm_res_ln_kernel(a_ref, w_ref, x_ref, g_ref, b_ref, of_ref, ob_ref, acc_ref):
    kk = pl.program_id(1)

    @pl.when(kk == 0)
    def _():
        acc_ref[...] = jnp.zeros_like(acc_ref)

    acc_ref[...] += _dot(a_ref[...], w_ref[...])

    @pl.when(kk == pl.num_programs(1) - 1)
    def _():
        y = ALPHA * x_ref[...] + acc_ref[...]
        mu = jnp.mean(y, axis=-1, keepdims=True)
        d = y - mu
        var = jnp.mean(d * d, axis=-1, keepdims=True)
        y = d * lax.rsqrt(var + LN_EPS) * g_ref[...] + b_ref[...]
        of_ref[...] = y
        ob_ref[...] = y.astype(BF16)


def _mm_res_ln(a_bf, w_bf, x_f32, g, b):
    m, k = a_bf.shape
    bm = min(512, m)
    bk = 512
    return pl.pallas_call(
        _mm_res_ln_kernel,
        grid=(m // bm, k // bk),
        in_specs=[
            pl.BlockSpec((bm, bk), lambda i, kk: (i, kk)),
            pl.BlockSpec((bk, D_MODEL), lambda i, kk: (kk, 0)),
            pl.BlockSpec((bm, D_MODEL), lambda i, kk: (i, 0)),
            pl.BlockSpec((1, D_MODEL), lambda i, kk: (0, 0)),
            pl.BlockSpec((1, D_MODEL), lambda i, kk: (0, 0)),
        ],
        out_specs=[
            pl.BlockSpec((bm, D_MODEL), lambda i, kk: (i, 0)),
            pl.BlockSpec((bm, D_MODEL), lambda i, kk: (i, 0)),
        ],
        out_shape=[
            jax.ShapeDtypeStruct((m, D_MODEL), F32),
            jax.ShapeDtypeStruct((m, D_MODEL), BF16),
        ],
        scratch_shapes=[pltpu.VMEM((bm, D_MODEL), F32)],
        compiler_params=_params(("parallel", "arbitrary")),
        name="mm_res_ln",
    )(a_bf, w_bf, x_f32, g.reshape(1, D_MODEL), b.reshape(1, D_MODEL))


def _rms(x, g):
    return x * lax.rsqrt(jnp.mean(x * x, axis=-1, keepdims=True) + RMS_EPS) * g


def _mla_in_kernel(x_ref, w_ref, qg_ref, kg_ref, tab_ref, cq_ref, ckv_ref, kr_ref):
    acc = _dot(x_ref[...], w_ref[...])
    cq_ref[...] = _rms(acc[:, :MLA_Q_RANK], qg_ref[...]).astype(BF16)
    ckv_ref[...] = _rms(acc[:, MLA_Q_RANK:MLA_Q_RANK + MLA_KV_RANK], kg_ref[...]).astype(BF16)
    part = acc[:, MLA_Q_RANK + MLA_KV_RANK:] * tab_ref[...]
    kr_ref[...] = (part + pltpu.roll(part, 64, axis=1)).astype(BF16)


def _mla_in_proj(x_bf, w_bf, q_norm, kv_norm, ktab, seq):
    m = x_bf.shape[0]
    bm = min(512, seq)
    n_pos_blocks = seq // bm
    return pl.pallas_call(
        _mla_in_kernel,
        grid=(m // bm,),
        in_specs=[
            pl.BlockSpec((bm, D_MODEL), lambda i: (i, 0)),
            pl.BlockSpec((D_MODEL, MLA_IN_W), lambda i: (0, 0)),
            pl.BlockSpec((1, MLA_Q_RANK), lambda i: (0, 0)),
            pl.BlockSpec((1, MLA_KV_RANK), lambda i: (0, 0)),
            pl.BlockSpec((bm, 128), lambda i: (i % n_pos_blocks, 0)),
        ],
        out_specs=[
            pl.BlockSpec((bm, MLA_Q_RANK), lambda i: (i, 0)),
            pl.BlockSpec((bm, MLA_KV_RANK), lambda i: (i, 0)),
            pl.BlockSpec((bm, 128), lambda i: (i, 0)),
        ],
        out_shape=[
            jax.ShapeDtypeStruct((m, MLA_Q_RANK), BF16),
            jax.ShapeDtypeStruct((m, MLA_KV_RANK), BF16),
            jax.ShapeDtypeStruct((m, 128), BF16),
        ],
        compiler_params=_params(("parallel",)),
        name="mla_in_proj",
    )(x_bf, w_bf, q_norm.reshape(1, -1), kv_norm.reshape(1, -1), ktab)


def _mla_q_kernel(c_ref, w_ref, tab_ref, o_ref, *, heads_per_tile):
    acc = _dot(c_ref[...], w_ref[...])
    tab = tab_ref[...]
    for h in range(heads_per_tile):
        lo = h * MLA_HEAD_W
        o_ref[:, lo:lo + MLA_HEAD_W] = (acc[:, lo:lo + MLA_HEAD_W] * tab).astype(BF16)


def _mla_q_proj(cq, w_bf, qtab, seq):
    m = cq.shape[0]
    bm = min(1024, seq)
    bn = 1024
    n_pos_blocks = seq // bm
    n_out = MLA_HEADS * MLA_HEAD_W
    return pl.pallas_call(
        functools.partial(_mla_q_kernel, heads_per_tile=bn // MLA_HEAD_W),
        grid=(m // bm, n_out // bn),
        in_specs=[
            pl.BlockSpec((bm, MLA_Q_RANK), lambda i, j: (i, 0)),
            pl.BlockSpec((MLA_Q_RANK, bn), lambda i, j: (0, j)),
            pl.BlockSpec((bm, MLA_HEAD_W), lambda i, j: (i % n_pos_blocks, 0)),
        ],
        out_specs=pl.BlockSpec((bm, bn), lambda i, j: (i, j)),
        out_shape=jax.ShapeDtypeStruct((m, n_out), BF16),
        compiler_params=_params(("parallel", "arbitrary")),
        name="mla_q_proj",
    )(cq, w_bf, qtab)


def _mla_kv_kernel(c_ref, w_ref, kr_ref, k_ref, v_ref):
    acc = _dot(c_ref[...], w_ref[...])
    kr = kr_ref[...]
    for h in range(MLA_HEADS):
        lo = h * MLA_HEAD_W
        k_ref[:, lo:lo + MLA_NOPE] = acc[:, h * MLA_NOPE:(h + 1) * MLA_NOPE].astype(BF16)
        k_ref[:, lo + MLA_NOPE:lo + MLA_HEAD_W] = kr
    v_ref[...] = acc[:, MLA_HEADS * MLA_NOPE:].astype(BF16)


def _mla_kv_proj(ckv, w_bf, kr):
    m = ckv.shape[0]
    bm = min(512, m)
    return pl.pallas_call(
        _mla_kv_kernel,
        grid=(m // bm,),
        in_specs=[
            pl.BlockSpec((bm, MLA_KV_RANK), lambda i: (i, 0)),
            pl.BlockSpec((MLA_KV_RANK, MLA_HEADS * (MLA_NOPE + MLA_V)), lambda i: (0, 0)),
            pl.BlockSpec((bm, 128), lambda i: (i, 0)),
        ],
        out_specs=[
            pl.BlockSpec((bm, MLA_HEADS * MLA_HEAD_W), lambda i: (i, 0)),
            pl.BlockSpec((bm, MLA_HEADS * MLA_V), lambda i: (i, 0)),
        ],
        out_shape=[
            jax.ShapeDtypeStruct((m, MLA_HEADS * MLA_HEAD_W), BF16),
            jax.ShapeDtypeStruct((m, MLA_HEADS * MLA_V), BF16),
        ],
        compiler_params=_params(("parallel",)),
        name="mla_kv_proj",
    )(ckv, w_bf, kr)


def _attn_kernel(q_ref, k_ref, v_ref, o_ref, *, kc):
    q = q_ref[0]
    bq = q.shape[0]
    n_kv = k_ref.shape[1] // kc

    def body(i, carry):
        m, l, acc = carry
        rows = pl.ds(pl.multiple_of(i * kc, kc), kc)
        s = _dot_nt(q, k_ref[0, rows, :])
        m_new = jnp.maximum(m, jnp.max(s, axis=-1, keepdims=True))
        alpha = jnp.exp2(m - m_new)
        p = jnp.exp2(s - m_new)
        l = alpha * l + jnp.sum(p, axis=-1, keepdims=True)
        acc = alpha * acc + _dot(p.astype(BF16), v_ref[0, rows, :])
        return m_new, l, acc

    m0 = jnp.full((bq, 1), -jnp.inf, F32)
    l0 = jnp.zeros((bq, 1), F32)
    a0 = jnp.zeros((bq, MLA_V), F32)
    _, l, acc = lax.fori_loop(0, n_kv, body, (m0, l0, a0))
    o_ref[0] = (acc / l).astype(BF16)


def _attention(q, k, v, batch, seq):
    bq = min(512, seq)
    kc = min(512, seq)
    q3 = q.reshape(batch, seq, MLA_HEADS * MLA_HEAD_W)
    k3 = k.reshape(batch, seq, MLA_HEADS * MLA_HEAD_W)
    v3 = v.reshape(batch, seq, MLA_HEADS * MLA_V)
    return pl.pallas_call(
        functools.partial(_attn_kernel, kc=kc),
        grid=(batch, MLA_HEADS, seq // bq),
        in_specs=[
            pl.BlockSpec((1, bq, MLA_HEAD_W), lambda b, h, i: (b, i, h)),
            pl.BlockSpec((1, seq, MLA_HEAD_W), lambda b, h, i: (b, 0, h)),
            pl.BlockSpec((1, seq, MLA_V), lambda b, h, i: (b, 0, h)),
        ],
        out_specs=pl.BlockSpec((1, bq, MLA_V), lambda b, h, i: (b, i, h)),
        out_shape=jax.ShapeDtypeStruct((batch, seq, MLA_HEADS * MLA_V), BF16),
        compiler_params=_params(("parallel", "parallel", "arbitrary")),
        name="attention",
    )(q3, k3, v3).reshape(batch * seq, MLA_HEADS * MLA_V)


def _ffn_halo_kernel(x_ref, w_ref, o_ref):
    o_ref[...] = _dot(x_ref[...], w_ref[...])


def _ffn_halo(x_rows, w_up_bf):
    r = x_rows.shape[0]
    bn = 512
    g_first = FFN_DIM // bn
    return pl.pallas_call(
        _ffn_halo_kernel,
        grid=(FFN_DIM // bn,),
        in_specs=[
            pl.BlockSpec((r, D_MODEL), lambda j: (0, 0)),
            pl.BlockSpec((D_MODEL, bn), lambda j: (0, g_first + j)),
        ],
        out_specs=pl.BlockSpec((r, bn), lambda j: (0, j)),
        out_shape=jax.ShapeDtypeStruct((r, FFN_DIM), F32),
        compiler_params=_params(("parallel",)),
        name="ffn_halo",
    )(x_rows, w_up_bf)


def _ffn_up_kernel(x_ref, wu_ref, wg_ref, cw_ref, cb_ref, hp_ref, hn_ref, o_ref):
    x = x_ref[...]
    u = _dot(x, wu_ref[...])
    g = _dot(x, wg_ref[...])
    bm = g.shape[0]
    row = lax.broadcasted_iota(jnp.int32, g.shape, 0)
    g_prev = jnp.where(row == 0, hp_ref[0], pltpu.roll(g, 1, axis=0))
    g_next = jnp.where(row == bm - 1, hn_ref[0], pltpu.roll(g, bm - 1, axis=0))
    cw = cw_ref[...]
    gc = g_prev * cw[0:1, :] + g * cw[1:2, :] + g_next * cw[2:3, :] + cb_ref[...]
    o_ref[...] = (gc * jax.nn.sigmoid(gc) * u).astype(BF16)


def _ffn_up(x_bf, w_up_bf, conv_w, conv_b, seq):
    m = x_bf.shape[0]
    bm = min(1024, seq)
    bn = 512
    nmb = m // bm
    g_first = FFN_DIM // bn
    xr = x_bf.reshape(nmb, bm, D_MODEL)
    tile_start = np.arange(nmb) * bm
    has_prev = jnp.asarray((tile_start % seq) != 0)[:, None]
    has_next = jnp.asarray(((tile_start + bm) % seq) != 0)[:, None]
    x_prev = jnp.where(has_prev, jnp.roll(xr[:, bm - 1, :], 1, axis=0), 0).astype(BF16)
    x_next = jnp.where(has_next, jnp.roll(xr[:, 0, :], -1, axis=0), 0).astype(BF16)
    pad = (-2 * nmb) % 16
    rows = jnp.concatenate([x_prev, x_next, jnp.zeros((pad, D_MODEL), BF16)], axis=0)
    halo = _ffn_halo(rows, w_up_bf)
    h_prev = halo[:nmb].reshape(nmb, 1, FFN_DIM)
    h_next = halo[nmb:2 * nmb].reshape(nmb, 1, FFN_DIM)
    return pl.pallas_call(
        _ffn_up_kernel,
        grid=(nmb, FFN_DIM // bn),
        in_specs=[
            pl.BlockSpec((bm, D_MODEL), lambda i, j: (i, 0)),
            pl.BlockSpec((D_MODEL, bn), lambda i, j: (0, j)),
            pl.BlockSpec((D_MODEL, bn), lambda i, j: (0, g_first + j)),
            pl.BlockSpec((3, bn), lambda i, j: (0, j)),
            pl.BlockSpec((1, bn), lambda i, j: (0, j)),
            pl.BlockSpec((1, 1, bn), lambda i, j: (i, 0, j)),
            pl.BlockSpec((1, 1, bn), lambda i, j: (i, 0, j)),
        ],
        out_specs=pl.BlockSpec((bm, bn), lambda i, j: (i, j)),
        out_shape=jax.ShapeDtypeStruct((m, FFN_DIM), BF16),
        compiler_params=_params(("parallel", "arbitrary")),
        name="ffn_up",
    )(x_bf, w_up_bf, w_up_bf, conv_w, conv_b.reshape(1, FFN_DIM), h_prev, h_next)


def _rope_tables(seq, dim, base):
    inv = 1.0 / (base ** (jnp.arange(0, dim // 2, dtype=F32) * (2.0 / dim)))
    ang = jnp.arange(seq, dtype=F32)[:, None] * inv[None, :]
    return jnp.cos(ang), jnp.sin(ang)


def _mla_column_orders():
    half = MLA_ROPE // 2
    rope_order = np.concatenate([np.arange(half), half + np.arange(half),
                                 half + np.arange(half), np.arange(half)])
    per_head = MLA_NOPE + MLA_ROPE
    q_cols = np.concatenate([
        np.concatenate([h * per_head + np.arange(MLA_NOPE), h * per_head + MLA_NOPE + rope_order])
        for h in range(MLA_HEADS)])
    in_cols = np.concatenate([np.arange(MLA_Q_RANK + MLA_KV_RANK),
                              MLA_Q_RANK + MLA_KV_RANK + rope_order])
    per_kv = MLA_NOPE + MLA_V
    kv_cols = np.concatenate(
        [h * per_kv + np.arange(MLA_NOPE) for h in range(MLA_HEADS)]
        + [h * per_kv + MLA_NOPE + np.arange(MLA_V) for h in range(MLA_HEADS)])
    return in_cols, q_cols, kv_cols


def _trunk(x, seq, w):
    batch = x.shape[0]
    m = batch * seq
    x_f = x.reshape(m, D_MODEL)
    x_b = x_f.astype(BF16)

    ret_cos, ret_sin = _rope_tables(seq, RET_QK_DIM, RET_ROPE_BASE)
    cos, sin = _rope_tables(seq, MLA_ROPE, MLA_ROPE_BASE)
    rot = jnp.concatenate([cos, cos, -sin, sin], axis=1)
    score_scale = (MLA_NOPE + MLA_ROPE) ** -0.5 * LOG2_E
    qtab = jnp.concatenate([jnp.ones((seq, MLA_NOPE), F32), rot], axis=1) * score_scale

    for i in range(DEPTH):
        j = i // 2
        if i % 2 == 0:
            p = _ret_in_proj(x_b, w["ret_w_in"][j], ret_cos, ret_sin, seq)
            mix = _retention(p, w["ret_decay_fwd"][j], w["ret_decay_bwd"][j], batch, seq)
            x_f, x_b = _mm_res_ln(mix, w["ret_w_out"][j], x_f, w["ln1_g"][i], w["ln1_b"][i])
        else:
            cq, ckv, kr = _mla_in_proj(x_b, w["mla_w_in"][j], w["mla_q_norm"][j],
                                       w["mla_kv_norm"][j], rot, seq)
            q = _mla_q_proj(cq, w["mla_w_uq"][j], qtab, seq)
            k, v = _mla_kv_proj(ckv, w["mla_w_ukv"][j], kr)
            mix = _attention(q, k, v, batch, seq)
            x_f, x_b = _mm_res_ln(mix, w["mla_w_out"][j], x_f, w["ln1_g"][i], w["ln1_b"][i])
        a = _ffn_up(x_b, w["ffn_w_up"][i], w["ffn_conv_w"][i], w["ffn_conv_b"][i], seq)
        x_f, x_b = _mm_res_ln(a, w["ffn_w_down"][i], x_f, w["ln2_g"][i], w["ln2_b"][i])
    return x_f.reshape(batch, seq, D_MODEL)


def _prepare_weights(ret_w_in, ret_decay_fwd, ret_decay_bwd, ret_w_out,
                     mla_w_in, mla_q_norm, mla_kv_norm, mla_w_uq, mla_w_ukv, mla_w_out,
                     ln1_g, ln1_b, ln2_g, ln2_b, ffn_w_up, ffn_conv_w, ffn_conv_b, ffn_w_down):
    in_cols, q_cols, kv_cols = _mla_column_orders()

    def lanes(d):
        return jnp.broadcast_to(d.astype(F32)[:, :, None, None], d.shape + (1, 128))

    return {
        "ret_w_in": ret_w_in.astype(BF16),
        "ret_decay_fwd": lanes(ret_decay_fwd),
        "ret_decay_bwd": lanes(ret_decay_bwd),
        "ret_w_out": ret_w_out.astype(BF16),
        "mla_w_in": mla_w_in[:, :, in_cols].astype(BF16),
        "mla_q_norm": mla_q_norm.astype(F32),
        "mla_kv_norm": mla_kv_norm.astype(F32),
        "mla_w_uq": mla_w_uq[:, :, q_cols].astype(BF16),
        "mla_w_ukv": mla_w_ukv[:, :, kv_cols].astype(BF16),
        "mla_w_out": mla_w_out.astype(BF16),
        "ln1_g": ln1_g.astype(F32), "ln1_b": ln1_b.astype(F32),
        "ln2_g": ln2_g.astype(F32), "ln2_b": ln2_b.astype(F32),
        "ffn_w_up": ffn_w_up.astype(BF16),
        "ffn_conv_w": ffn_conv_w.astype(F32),
        "ffn_conv_b": ffn_conv_b.astype(F32),
        "ffn_w_down": ffn_w_down.astype(BF16),
    }


def kernel(x_prompt, x_sample, ret_w_in, ret_decay_fwd, ret_decay_bwd, ret_w_out, mla_w_in, mla_q_norm, mla_kv_norm, mla_w_uq, mla_w_ukv, mla_w_out, ln1_g, ln1_b, ln2_g, ln2_b, ffn_w_up, ffn_conv_w, ffn_conv_b, ffn_w_down):
    w = _prepare_weights(ret_w_in, ret_decay_fwd, ret_decay_bwd, ret_w_out,
                         mla_w_in, mla_q_norm, mla_kv_norm, mla_w_uq, mla_w_ukv, mla_w_out,
                         ln1_g, ln1_b, ln2_g, ln2_b, ffn_w_up, ffn_conv_w, ffn_conv_b, ffn_w_down)
    y_prompt = _trunk(x_prompt, x_prompt.shape[1], w)
    y_sample = _trunk(x_sample, x_sample.shape[1], w)
    return (y_prompt, y_sample)
```

```python
import functools
import math

import jax
import jax.numpy as jnp
import numpy as np
from jax import lax
from jax.experimental import pallas as pl
from jax.experimental.pallas import tpu as pltpu

F32 = jnp.float32
BF16 = jnp.bfloat16

D_MODEL = 2048
DEPTH = 4
ALPHA = (2.0 * DEPTH) ** 0.25

RET_HEADS = 8
RET_QK_DIM = 256
RET_V_DIM = 512
RET_QK_W = RET_HEADS * RET_QK_DIM
RET_V_W = RET_HEADS * RET_V_DIM
RET_IN = 2 * RET_QK_W + 2 * RET_V_W
RET_ROPE_BASE = 10000.0

MLA_HEADS = 16
MLA_Q_RANK = 1536
MLA_KV_RANK = 512
MLA_NOPE = 128
MLA_ROPE = 64
MLA_V = 128
MLA_ROPE_BASE = 10000.0
MLA_HEAD_W = 256
MLA_IN_W = MLA_Q_RANK + MLA_KV_RANK + 128
MLA_V_W = 256

FFN_DIM = 5632
FFN_SLAB = 256

LN_EPS = 1e-5
RMS_EPS = 1e-6
GN_EPS = 1e-6

VMEM_LIMIT_BYTES = 56 * 1024 * 1024
LOG2_E = 1.4426950408889634


def _params(semantics):
    return pltpu.CompilerParams(dimension_semantics=semantics,
                                vmem_limit_bytes=VMEM_LIMIT_BYTES)


def _dot(a, b):
    return jnp.dot(a, b, preferred_element_type=F32)


def _dot_nt(a, b):
    return lax.dot_general(a, b, (((1,), (1,)), ((), ())), preferred_element_type=F32)


def _dot_tn(a, b):
    return lax.dot_general(a, b, (((0,), (0,)), ((), ())), preferred_element_type=F32)


def _ret_in_kernel(x_ref, w_ref, cos_ref, sin_ref, o_ref, *, bn):
    j = pl.program_id(1)
    n_qk_tiles = 2 * RET_QK_W // bn

    @pl.when(j < n_qk_tiles)
    def _():
        cos = cos_ref[...]
        sin = sin_ref[...]
        scale = jnp.where(j >= n_qk_tiles // 2, RET_QK_DIM ** -0.5, 1.0).astype(F32)
        half = RET_QK_DIM // 2
        for h in range(bn // RET_QK_DIM):
            lo = h * RET_QK_DIM
            acc = _dot(x_ref[...], w_ref[:, lo:lo + RET_QK_DIM])
            x1 = acc[:, :half]
            x2 = acc[:, half:]
            o_ref[:, lo:lo + half] = ((x1 * cos - x2 * sin) * scale).astype(BF16)
            o_ref[:, lo + half:lo + 2 * half] = ((x1 * sin + x2 * cos) * scale).astype(BF16)

    @pl.when(j >= n_qk_tiles)
    def _():
        for h in range(bn // RET_QK_DIM):
            cols = slice(h * RET_QK_DIM, (h + 1) * RET_QK_DIM)
            o_ref[:, cols] = _dot(x_ref[...], w_ref[:, cols]).astype(BF16)


def _ret_in_proj(x_bf, w_bf, layer, cos, sin, seq):
    m = x_bf.shape[0]
    bm = min(1024, seq)
    bn = 1024
    n_pos_blocks = seq // bm
    return pl.pallas_call(
        functools.partial(_ret_in_kernel, bn=bn),
        grid=(m // bm, RET_IN // bn),
        in_specs=[
            pl.BlockSpec((bm, D_MODEL), lambda i, j: (i, 0)),
            pl.BlockSpec((None, D_MODEL, bn), lambda i, j: (layer, 0, j)),
            pl.BlockSpec((bm, RET_QK_DIM // 2), lambda i, j: (i % n_pos_blocks, 0)),
            pl.BlockSpec((bm, RET_QK_DIM // 2), lambda i, j: (i % n_pos_blocks, 0)),
        ],
        out_specs=pl.BlockSpec((bm, bn), lambda i, j: (i, j)),
        out_shape=jax.ShapeDtypeStruct((m, RET_IN), BF16),
        compiler_params=_params(("parallel", "arbitrary")),
        name="ret_in_proj",
    )(x_bf, w_bf, cos, sin)


def _log_sigmoid(x):
    return jnp.minimum(x, 0.0) - jnp.log1p(jnp.exp(-jnp.abs(x)))


def _retention_kernel(q_ref, k_ref, v_ref, g_ref, df_ref, db_ref, o_ref,
                      state_ref, obuf_ref, dmat_ref, xi_ref, zeta_ref, *, chunk, n_sub):
    ps = pl.program_id(2)
    t = pl.program_id(3)
    nt = pl.num_programs(3)
    c = chunk

    @pl.when(t == 0)
    def _():
        state_ref[...] = jnp.zeros_like(state_ref)

    @pl.when((t == 0) & (ps == 0))
    def _():
        lf = _log_sigmoid(df_ref[...])
        lb = _log_sigmoid(db_ref[...])
        row = lax.broadcasted_iota(jnp.int32, (c, 128), 0).astype(F32)
        xi_ref[0] = jnp.exp(lb * (c - row))
        xi_ref[1] = jnp.exp(lf * (row + 1.0))
        zeta_ref[0] = jnp.exp(lb * row)
        zeta_ref[1] = jnp.exp(lf * (c - 1.0 - row))
        for jb in range(c // 128):
            col = lax.broadcasted_iota(jnp.int32, (c, 128), 1).astype(F32) + (128.0 * jb)
            rel = row - col
            dmat_ref[:, jb * 128:(jb + 1) * 128] = jnp.where(
                rel >= 0.0, jnp.exp(lf * jnp.maximum(rel, 0.0)), jnp.exp(lb * jnp.maximum(-rel, 0.0)))

    def tile128(tab, width):
        return jnp.concatenate([tab] * (width // 128), axis=1)

    def state_step(qc, kc, vc, direction):
        xi = xi_ref[direction]
        zeta = zeta_ref[direction]
        st = state_ref[...]
        inter = _dot(qc, st.astype(BF16)) * tile128(xi, RET_V_DIM)
        kz = (kc.astype(F32) * tile128(zeta, RET_QK_DIM)).astype(BF16)
        g_chunk = xi[c - 1:c, :] if direction == 1 else xi[0:1, :]
        state_ref[...] = st * tile128(g_chunk, RET_V_DIM) + _dot_tn(kz, vc)
        return inter

    @pl.when(ps == 0)
    def _():
        for s in reversed(range(n_sub)):
            rows = pl.ds(s * c, c)
            inter = state_step(q_ref[0, rows, :], k_ref[0, rows, :], v_ref[0, rows, :], 0)
            base = pl.multiple_of(((nt - 1 - t) * n_sub + s) * c, c)
            obuf_ref[pl.ds(base, c), :] = inter

    @pl.when(ps == 1)
    def _():
        for s in range(n_sub):
            rows = pl.ds(s * c, c)
            qc = q_ref[0, rows, :]
            kc = k_ref[0, rows, :]
            vc = v_ref[0, rows, :]
            sc = (_dot_nt(qc, kc) * dmat_ref[...]).astype(BF16)
            base = pl.multiple_of((t * n_sub + s) * c, c)
            o = _dot(sc, vc) + obuf_ref[pl.ds(base, c), :]
            o = o + state_step(qc, kc, vc, 1)
            mu = jnp.mean(o, axis=-1, keepdims=True)
            d = o - mu
            var = jnp.mean(d * d, axis=-1, keepdims=True)
            o = d * lax.rsqrt(var + GN_EPS)
            gate = g_ref[0, rows, :].astype(F32)
            o_ref[0, rows, :] = (gate * jax.nn.sigmoid(gate) * o).astype(BF16)


def _retention(p, decay_f, decay_b, layer, batch, seq):
    chunk = min(256, seq)
    tb = min(1024, seq)
    n_sub = tb // chunk
    nt = seq // tb
    p3 = p.reshape(batch, seq, RET_IN)
    qk_blocks = RET_QK_W // RET_QK_DIM
    v_first = 2 * RET_QK_W // RET_V_DIM
    g_first = v_first + RET_HEADS

    def pos(ps, t):
        return jnp.where(ps == 0, nt - 1 - t, t)

    return pl.pallas_call(
        functools.partial(_retention_kernel, chunk=chunk, n_sub=n_sub),
        grid=(batch, RET_HEADS, 2, nt),
        in_specs=[
            pl.BlockSpec((1, tb, RET_QK_DIM), lambda b, h, ps, t: (b, pos(ps, t), h)),
            pl.BlockSpec((1, tb, RET_QK_DIM), lambda b, h, ps, t: (b, pos(ps, t), qk_blocks + h)),
            pl.BlockSpec((1, tb, RET_V_DIM), lambda b, h, ps, t: (b, pos(ps, t), v_first + h)),
            pl.BlockSpec((1, tb, RET_V_DIM), lambda b, h, ps, t: (b, t * ps, g_first + h)),
            pl.BlockSpec((None, None, 1, 128), lambda b, h, ps, t: (layer, h, 0, 0)),
            pl.BlockSpec((None, None, 1, 128), lambda b, h, ps, t: (layer, h, 0, 0)),
        ],
        out_specs=pl.BlockSpec((1, tb, RET_V_DIM), lambda b, h, ps, t: (b, t * ps, h)),
        out_shape=jax.ShapeDtypeStruct((batch, seq, RET_V_W), BF16),
        scratch_shapes=[
            pltpu.VMEM((RET_QK_DIM, RET_V_DIM), F32),
            pltpu.VMEM((seq, RET_V_DIM), F32),
            pltpu.VMEM((chunk, chunk), F32),
            pltpu.VMEM((2, chunk, 128), F32),
            pltpu.VMEM((2, chunk, 128), F32),
        ],
        compiler_params=_params(("parallel", "parallel", "arbitrary", "arbitrary")),
        name="retention",
    )(p3, p3, p3, p3, decay_f, decay_b).reshape(batch * seq, RET_V_W)


def _mm_res_ln_kernel(a_ref, w_ref, x_ref, g_ref, b_ref, of_ref, ob_ref, acc_ref):
    kk = pl.program_id(1)
    last = pl.num_programs(1) - 1

    @pl.when(kk == 0)
    def _():
        acc_ref[...] = _dot(a_ref[...], w_ref[...])

    @pl.when((kk > 0) & (kk < last))
    def _():
        acc_ref[...] += _dot(a_ref[...], w_ref[...])

    @pl.when(kk == last)
    def _():
        y = ALPHA * x_ref[...] + (acc_ref[...] + _dot(a_ref[...], w_ref[...]))
        mu = jnp.mean(y, axis=-1, keepdims=True)
        d = y - mu
        var = jnp.mean(d * d, axis=-1, keepdims=True)
        y = d * lax.rsqrt(var + LN_EPS) * g_ref[...] + b_ref[...]
        of_ref[...] = y
        ob_ref[...] = y.astype(BF16)


def _mm_res_ln(a_bf, w_bf, layer, x_f32, g, b):
    m, k = a_bf.shape
    bm = min(512, m)
    bk = 512
    assert k // bk >= 2
    return pl.pallas_call(
        _mm_res_ln_kernel,
        grid=(m // bm, k // bk),
        in_specs=[
            pl.BlockSpec((bm, bk), lambda i, kk: (i, kk)),
            pl.BlockSpec((None, bk, D_MODEL), lambda i, kk: (layer, kk, 0)),
            pl.BlockSpec((bm, D_MODEL), lambda i, kk: (i, 0)),
            pl.BlockSpec((1, D_MODEL), lambda i, kk: (0, 0)),
            pl.BlockSpec((1, D_MODEL), lambda i, kk: (0, 0)),
        ],
        out_specs=[
            pl.BlockSpec((bm, D_MODEL), lambda i, kk: (i, 0)),
            pl.BlockSpec((bm, D_MODEL), lambda i, kk: (i, 0)),
        ],
        out_shape=[
            jax.ShapeDtypeStruct((m, D_MODEL), F32),
            jax.ShapeDtypeStruct((m, D_MODEL), BF16),
        ],
        scratch_shapes=[pltpu.VMEM((bm, D_MODEL), F32)],
        compiler_params=_params(("parallel", "arbitrary")),
        name="mm_res_ln",
    )(a_bf, w_bf, x_f32, g.reshape(1, D_MODEL), b.reshape(1, D_MODEL))


def _rms(x, g):
    return x * lax.rsqrt(jnp.mean(x * x, axis=-1, keepdims=True) + RMS_EPS) * g


def _mla_in_kernel(x_ref, w_ref, qg_ref, kg_ref, tab_ref, cq_ref, ckv_ref, kr_ref):
    acc = _dot(x_ref[...], w_ref[...])
    cq_ref[...] = _rms(acc[:, :MLA_Q_RANK], qg_ref[...]).astype(BF16)
    ckv_ref[...] = _rms(acc[:, MLA_Q_RANK:MLA_Q_RANK + MLA_KV_RANK], kg_ref[...]).astype(BF16)
    part = acc[:, MLA_Q_RANK + MLA_KV_RANK:] * tab_ref[...]
    kr_ref[...] = (part + pltpu.roll(part, 64, axis=1)).astype(BF16)


def _mla_in_proj(x_bf, w_bf, layer, q_norm, kv_norm, ktab, seq):
    m = x_bf.shape[0]
    bm = min(512, seq)
    n_pos_blocks = seq // bm
    return pl.pallas_call(
        _mla_in_kernel,
        grid=(m // bm,),
        in_specs=[
            pl.BlockSpec((bm, D_MODEL), lambda i: (i, 0)),
            pl.BlockSpec((None, D_MODEL, MLA_IN_W), lambda i: (layer, 0, 0)),
            pl.BlockSpec((1, MLA_Q_RANK), lambda i: (0, 0)),
            pl.BlockSpec((1, MLA_KV_RANK), lambda i: (0, 0)),
            pl.BlockSpec((bm, 128), lambda i: (i % n_pos_blocks, 0)),
        ],
        out_specs=[
            pl.BlockSpec((bm, MLA_Q_RANK), lambda i: (i, 0)),
            pl.BlockSpec((bm, MLA_KV_RANK), lambda i: (i, 0)),
            pl.BlockSpec((bm, 128), lambda i: (i, 0)),
        ],
        out_shape=[
            jax.ShapeDtypeStruct((m, MLA_Q_RANK), BF16),
            jax.ShapeDtypeStruct((m, MLA_KV_RANK), BF16),
            jax.ShapeDtypeStruct((m, 128), BF16),
        ],
        compiler_params=_params(("parallel",)),
        name="mla_in_proj",
    )(x_bf, w_bf, q_norm.reshape(1, -1), kv_norm.reshape(1, -1), ktab)


def _mla_q_kernel(c_ref, w_ref, tab_ref, o_ref, *, heads_per_tile):
    acc = _dot(c_ref[...], w_ref[...])
    tab = tab_ref[...]
    for h in range(heads_per_tile):
        lo = h * MLA_HEAD_W
        o_ref[:, lo:lo + MLA_HEAD_W] = (acc[:, lo:lo + MLA_HEAD_W] * tab).astype(BF16)


def _mla_q_proj(cq, w_bf, layer, qtab, seq):
    m = cq.shape[0]
    bm = min(1024, seq)
    bn = 1024
    n_pos_blocks = seq // bm
    n_out = MLA_HEADS * MLA_HEAD_W
    return pl.pallas_call(
        functools.partial(_mla_q_kernel, heads_per_tile=bn // MLA_HEAD_W),
        grid=(m // bm, n_out // bn),
        in_specs=[
            pl.BlockSpec((bm, MLA_Q_RANK), lambda i, j: (i, 0)),
            pl.BlockSpec((None, MLA_Q_RANK, bn), lambda i, j: (layer, 0, j)),
            pl.BlockSpec((bm, MLA_HEAD_W), lambda i, j: (i % n_pos_blocks, 0)),
        ],
        out_specs=pl.BlockSpec((bm, bn), lambda i, j: (i, j)),
        out_shape=jax.ShapeDtypeStruct((m, n_out), BF16),
        compiler_params=_params(("parallel", "arbitrary")),
        name="mla_q_proj",
    )(cq, w_bf, qtab)


def _mla_kv_kernel(c_ref, w_ref, kr_ref, k_ref, v_ref):
    acc = _dot(c_ref[...], w_ref[...])
    kr = kr_ref[...]
    ones = jnp.ones((acc.shape[0], MLA_V_W - MLA_V), BF16)
    v_first = MLA_HEADS * MLA_NOPE
    for h in range(MLA_HEADS):
        lo = h * MLA_HEAD_W
        k_ref[:, lo:lo + MLA_NOPE] = acc[:, h * MLA_NOPE:(h + 1) * MLA_NOPE].astype(BF16)
        k_ref[:, lo + MLA_NOPE:lo + MLA_HEAD_W] = kr
        lo = h * MLA_V_W
        v_ref[:, lo:lo + MLA_V] = acc[:, v_first + h * MLA_V:v_first + (h + 1) * MLA_V].astype(BF16)
        v_ref[:, lo + MLA_V:lo + MLA_V_W] = ones


def _mla_kv_proj(ckv, w_bf, layer, kr):
    m = ckv.shape[0]
    bm = min(512, m)
    return pl.pallas_call(
        _mla_kv_kernel,
        grid=(m // bm,),
        in_specs=[
            pl.BlockSpec((bm, MLA_KV_RANK), lambda i: (i, 0)),
            pl.BlockSpec((None, MLA_KV_RANK, MLA_HEADS * (MLA_NOPE + MLA_V)), lambda i: (layer, 0, 0)),
            pl.BlockSpec((bm, 128), lambda i: (i, 0)),
        ],
        out_specs=[
            pl.BlockSpec((bm, MLA_HEADS * MLA_HEAD_W), lambda i: (i, 0)),
            pl.BlockSpec((bm, MLA_HEADS * MLA_V_W), lambda i: (i, 0)),
        ],
        out_shape=[
            jax.ShapeDtypeStruct((m, MLA_HEADS * MLA_HEAD_W), BF16),
            jax.ShapeDtypeStruct((m, MLA_HEADS * MLA_V_W), BF16),
        ],
        compiler_params=_params(("parallel",)),
        name="mla_kv_proj",
    )(ckv, w_bf, kr)


def _attn_kernel(q_ref, k_ref, v_ref, o_ref, *, kc, group):
    q = q_ref[0]
    bq = q.shape[0]
    n_groups = k_ref.shape[1] // (kc * group)

    def kv_rows(g, c):
        return pl.ds(pl.multiple_of((g * group + c) * kc, kc), kc)

    def body(g, carry):
        m, acc = carry
        s_next = _dot_nt(q, k_ref[0, kv_rows(g, 0), :])
        for c in range(group):
            s = s_next
            if c + 1 < group:
                s_next = _dot_nt(q, k_ref[0, kv_rows(g, c + 1), :])
            m_new = jnp.maximum(m, jnp.max(s, axis=-1, keepdims=True))
            alpha = jnp.exp2(m - m_new)
            p = jnp.exp2(s - m_new).astype(BF16)
            acc = alpha * acc + _dot(p, v_ref[0, kv_rows(g, c), :])
            m = m_new
        return m, acc

    m0 = jnp.full((bq, 1), -jnp.inf, F32)
    a0 = jnp.zeros((bq, MLA_V_W), F32)
    _, acc = lax.fori_loop(0, n_groups, body, (m0, a0))
    o_ref[0] = (acc[:, :MLA_V] / acc[:, MLA_V:]).astype(BF16)


def _attention(q, k, v, batch, seq):
    bq = min(1024, seq)
    kc = min(512, seq)
    group = min(16, seq // kc)
    q3 = q.reshape(batch, seq, MLA_HEADS * MLA_HEAD_W)
    k3 = k.reshape(batch, seq, MLA_HEADS * MLA_HEAD_W)
    v3 = v.reshape(batch, seq, MLA_HEADS * MLA_V_W)
    return pl.pallas_call(
        functools.partial(_attn_kernel, kc=kc, group=group),
        grid=(batch, MLA_HEADS, seq // bq),
        in_specs=[
            pl.BlockSpec((1, bq, MLA_HEAD_W), lambda b, h, i: (b, i, h)),
            pl.BlockSpec((1, seq, MLA_HEAD_W), lambda b, h, i: (b, 0, h)),
            pl.BlockSpec((1, seq, MLA_V_W), lambda b, h, i: (b, 0, h)),
        ],
        out_specs=pl.BlockSpec((1, bq, MLA_V), lambda b, h, i: (b, i, h)),
        out_shape=jax.ShapeDtypeStruct((batch, seq, MLA_HEADS * MLA_V), BF16),
        compiler_params=_params(("parallel", "parallel", "arbitrary")),
        name="attention",
    )(q3, k3, v3).reshape(batch * seq, MLA_HEADS * MLA_V)


def _ffn_halo_kernel(x_ref, w_ref, o_ref):
    o_ref[...] = _dot(x_ref[...], w_ref[...])


def _ffn_halo(x_rows, w_up_bf, layer):
    r = x_rows.shape[0]
    bn = 512
    g_first = FFN_DIM // bn
    return pl.pallas_call(
        _ffn_halo_kernel,
        grid=(FFN_DIM // bn,),
        in_specs=[
            pl.BlockSpec((r, D_MODEL), lambda j: (0, 0)),
            pl.BlockSpec((None, D_MODEL, bn), lambda j: (layer, 0, g_first + j)),
        ],
        out_specs=pl.BlockSpec((r, bn), lambda j: (0, j)),
        out_shape=jax.ShapeDtypeStruct((r, FFN_DIM), F32),
        compiler_params=_params(("parallel",)),
        name="ffn_halo",
    )(x_rows, w_up_bf)


def _ffn_up_kernel(x_ref, wu_ref, wg_ref, cw_ref, cb_ref, hp_ref, hn_ref, o_ref):
    x = x_ref[...]
    bm = x.shape[0]
    for c in range(o_ref.shape[1] // FFN_SLAB):
        cols = slice(c * FFN_SLAB, (c + 1) * FFN_SLAB)
        u = _dot(x, wu_ref[:, cols])
        g = _dot(x, wg_ref[:, cols])
        row = lax.broadcasted_iota(jnp.int32, g.shape, 0)
        g_prev = jnp.where(row == 0, hp_ref[0, :, cols], pltpu.roll(g, 1, axis=0))
        g_next = jnp.where(row == bm - 1, hn_ref[0, :, cols], pltpu.roll(g, bm - 1, axis=0))
        gc = (g_prev * cw_ref[0:1, cols] + g * cw_ref[1:2, cols] + g_next * cw_ref[2:3, cols]
              + cb_ref[:, cols])
        o_ref[:, cols] = (gc * jax.nn.sigmoid(gc) * u).astype(BF16)


def _ffn_up(x_bf, w_up_bf, layer, conv_w, conv_b, seq):
    m = x_bf.shape[0]
    bm = min(1024, seq)
    bn = 512
    nmb = m // bm
    g_first = FFN_DIM // bn
    xr = x_bf.reshape(nmb, bm, D_MODEL)
    tile_start = np.arange(nmb) * bm
    has_prev = jnp.asarray((tile_start % seq) != 0)[:, None]
    has_next = jnp.asarray(((tile_start + bm) % seq) != 0)[:, None]
    x_prev = jnp.where(has_prev, jnp.roll(xr[:, bm - 1, :], 1, axis=0), 0).astype(BF16)
    x_next = jnp.where(has_next, jnp.roll(xr[:, 0, :], -1, axis=0), 0).astype(BF16)
    pad = (-2 * nmb) % 16
    rows = jnp.concatenate([x_prev, x_next, jnp.zeros((pad, D_MODEL), BF16)], axis=0)
    halo = _ffn_halo(rows, w_up_bf, layer)
    h_prev = halo[:nmb].reshape(nmb, 1, FFN_DIM)
    h_next = halo[nmb:2 * nmb].reshape(nmb, 1, FFN_DIM)
    return pl.pallas_call(
        _ffn_up_kernel,
        grid=(nmb, FFN_DIM // bn),
        in_specs=[
            pl.BlockSpec((bm, D_MODEL), lambda i, j: (i, 0)),
            pl.BlockSpec((None, D_MODEL, bn), lambda i, j: (layer, 0, j)),
            pl.BlockSpec((None, D_MODEL, bn), lambda i, j: (layer, 0, g_first + j)),
            pl.BlockSpec((3, bn), lambda i, j: (0, j)),
            pl.BlockSpec((1, bn), lambda i, j: (0, j)),
            pl.BlockSpec((1, 1, bn), lambda i, j: (i, 0, j)),
            pl.BlockSpec((1, 1, bn), lambda i, j: (i, 0, j)),
        ],
        out_specs=pl.BlockSpec((bm, bn), lambda i, j: (i, j)),
        out_shape=jax.ShapeDtypeStruct((m, FFN_DIM), BF16),
        compiler_params=_params(("parallel", "arbitrary")),
        name="ffn_up",
    )(x_bf, w_up_bf, w_up_bf, conv_w, conv_b.reshape(1, FFN_DIM), h_prev, h_next)


def _rope_tables(seq, dim, base):
    inv = 1.0 / (base ** (jnp.arange(0, dim // 2, dtype=F32) * (2.0 / dim)))
    ang = jnp.arange(seq, dtype=F32)[:, None] * inv[None, :]
    return jnp.cos(ang), jnp.sin(ang)


def _mla_column_orders():
    half = MLA_ROPE // 2
    rope_order = np.concatenate([np.arange(half), half + np.arange(half),
                                 half + np.arange(half), np.arange(half)])
    per_head = MLA_NOPE + MLA_ROPE
    q_cols = np.concatenate([
        np.concatenate([h * per_head + np.arange(MLA_NOPE), h * per_head + MLA_NOPE + rope_order])
        for h in range(MLA_HEADS)])
    in_cols = np.concatenate([np.arange(MLA_Q_RANK + MLA_KV_RANK),
                              MLA_Q_RANK + MLA_KV_RANK + rope_order])
    per_kv = MLA_NOPE + MLA_V
    kv_cols = np.concatenate(
        [h * per_kv + np.arange(MLA_NOPE) for h in range(MLA_HEADS)]
        + [h * per_kv + MLA_NOPE + np.arange(MLA_V) for h in range(MLA_HEADS)])
    return in_cols, q_cols, kv_cols


def _trunk(x, seq, w):
    batch = x.shape[0]
    m = batch * seq
    x_f = x.reshape(m, D_MODEL)
    x_b = x_f.astype(BF16)

    ret_cos, ret_sin = _rope_tables(seq, RET_QK_DIM, RET_ROPE_BASE)
    cos, sin = _rope_tables(seq, MLA_ROPE, MLA_ROPE_BASE)
    rot = jnp.concatenate([cos, cos, -sin, sin], axis=1)
    score_scale = (MLA_NOPE + MLA_ROPE) ** -0.5 * LOG2_E
    qtab = jnp.concatenate([jnp.ones((seq, MLA_NOPE), F32), rot], axis=1) * score_scale

    for i in range(DEPTH):
        j = i // 2
        if i % 2 == 0:
            p = _ret_in_proj(x_b, w["ret_w_in"], j, ret_cos, ret_sin, seq)
            mix = _retention(p, w["ret_decay_fwd"], w["ret_decay_bwd"], j, batch, seq)
            x_f, x_b = _mm_res_ln(mix, w["ret_w_out"], j, x_f, w["ln1_g"][i], w["ln1_b"][i])
        else:
            cq, ckv, kr = _mla_in_proj(x_b, w["mla_w_in"], j, w["mla_q_norm"][j],
                                       w["mla_kv_norm"][j], rot, seq)
            q = _mla_q_proj(cq, w["mla_w_uq"], j, qtab, seq)
            k, v = _mla_kv_proj(ckv, w["mla_w_ukv"], j, kr)
            mix = _attention(q, k, v, batch, seq)
            x_f, x_b = _mm_res_ln(mix, w["mla_w_out"], j, x_f, w["ln1_g"][i], w["ln1_b"][i])
        a = _ffn_up(x_b, w["ffn_w_up"], i, w["ffn_conv_w"][i], w["ffn_conv_b"][i], seq)
        x_f, x_b = _mm_res_ln(a, w["ffn_w_down"], i, x_f, w["ln2_g"][i], w["ln2_b"][i])
    return x_f.reshape(batch, seq, D_MODEL)


def _prepare_weights(ret_w_in, ret_decay_fwd, ret_decay_bwd, ret_w_out,
                     mla_w_in, mla_q_norm, mla_kv_norm, mla_w_uq, mla_w_ukv, mla_w_out,
                     ln1_g, ln1_b, ln2_g, ln2_b, ffn_w_up, ffn_conv_w, ffn_conv_b, ffn_w_down):
    in_cols, q_cols, kv_cols = _mla_column_orders()

    def lanes(d):
        return jnp.broadcast_to(d.astype(F32)[:, :, None, None], d.shape + (1, 128))

    return {
        "ret_w_in": ret_w_in.astype(BF16),
        "ret_decay_fwd": lanes(ret_decay_fwd),
        "ret_decay_bwd": lanes(ret_decay_bwd),
        "ret_w_out": ret_w_out.astype(BF16),
        "mla_w_in": mla_w_in[:, :, in_cols].astype(BF16),
        "mla_q_norm": mla_q_norm.astype(F32),
        "mla_kv_norm": mla_kv_norm.astype(F32),
        "mla_w_uq": mla_w_uq[:, :, q_cols].astype(BF16),
        "mla_w_ukv": mla_w_ukv[:, :, kv_cols].astype(BF16),
        "mla_w_out": mla_w_out.astype(BF16),
        "ln1_g": ln1_g.astype(F32), "ln1_b": ln1_b.astype(F32),
        "ln2_g": ln2_g.astype(F32), "ln2_b": ln2_b.astype(F32),
        "ffn_w_up": ffn_w_up.astype(BF16),
        "ffn_conv_w": ffn_conv_w.astype(F32),
        "ffn_conv_b": ffn_conv_b.astype(F32),
        "ffn_w_down": ffn_w_down.astype(BF16),
    }


def kernel(x_prompt, x_sample, ret_w_in, ret_decay_fwd, ret_decay_bwd, ret_w_out, mla_w_in, mla_q_norm, mla_kv_norm, mla_w_uq, mla_w_ukv, mla_w_out, ln1_g, ln1_b, ln2_g, ln2_b, ffn_w_up, ffn_conv_w, ffn_conv_b, ffn_w_down):
    w = _prepare_weights(ret_w_in, ret_decay_fwd, ret_decay_bwd, ret_w_out,
                         mla_w_in, mla_q_norm, mla_kv_norm, mla_w_uq, mla_w_ukv, mla_w_out,
                         ln1_g, ln1_b, ln2_g, ln2_b, ffn_w_up, ffn_conv_w, ffn_conv_b, ffn_w_down)
    y_prompt = _trunk(x_prompt, x_prompt.shape[1], w)
    y_sample = _trunk(x_sample, x_sample.shape[1], w)
    return (y_prompt, y_sample)
```

```python
import functools
import math

import jax
import jax.numpy as jnp
import numpy as np
from jax import lax
from jax.experimental import pallas as pl
from jax.experimental.pallas import tpu as pltpu

F32 = jnp.float32
BF16 = jnp.bfloat16

D_MODEL = 2048
DEPTH = 4
ALPHA = (2.0 * DEPTH) ** 0.25

RET_HEADS = 8
RET_QK_DIM = 256
RET_V_DIM = 512
RET_QK_W = RET_HEADS * RET_QK_DIM
RET_V_W = RET_HEADS * RET_V_DIM
RET_IN = 2 * RET_QK_W + 2 * RET_V_W
RET_ROPE_BASE = 10000.0

MLA_HEADS = 16
MLA_Q_RANK = 1536
MLA_KV_RANK = 512
MLA_NOPE = 128
MLA_ROPE = 64
MLA_V = 128
MLA_ROPE_BASE = 10000.0
MLA_HEAD_W = 256
MLA_IN_W = MLA_Q_RANK + MLA_KV_RANK + 128
MLA_V_W = 256

FFN_DIM = 5632
FFN_SLAB = 256

LN_EPS = 1e-5
RMS_EPS = 1e-6
GN_EPS = 1e-6

VMEM_LIMIT_BYTES = 56 * 1024 * 1024
LOG2_E = 1.4426950408889634


def _params(semantics):
    return pltpu.CompilerParams(dimension_semantics=semantics,
                                vmem_limit_bytes=VMEM_LIMIT_BYTES)


def _dot(a, b):
    return jnp.dot(a, b, preferred_element_type=F32)


def _dot_nt(a, b):
    return lax.dot_general(a, b, (((1,), (1,)), ((), ())), preferred_element_type=F32)


def _dot_tn(a, b):
    return lax.dot_general(a, b, (((0,), (0,)), ((), ())), preferred_element_type=F32)


def _ret_in_kernel(x_ref, w_ref, cos_ref, sin_ref, o_ref, *, bn):
    j = pl.program_id(1)
    n_qk_tiles = 2 * RET_QK_W // bn

    @pl.when(j < n_qk_tiles)
    def _():
        cos = cos_ref[...]
        sin = sin_ref[...]
        scale = jnp.where(j >= n_qk_tiles // 2, RET_QK_DIM ** -0.5, 1.0).astype(F32)
        half = RET_QK_DIM // 2
        for h in range(bn // RET_QK_DIM):
            lo = h * RET_QK_DIM
            acc = _dot(x_ref[...], w_ref[:, lo:lo + RET_QK_DIM])
            x1 = acc[:, :half]
            x2 = acc[:, half:]
            o_ref[:, lo:lo + half] = ((x1 * cos - x2 * sin) * scale).astype(BF16)
            o_ref[:, lo + half:lo + 2 * half] = ((x1 * sin + x2 * cos) * scale).astype(BF16)

    @pl.when(j >= n_qk_tiles)
    def _():
        for h in range(bn // RET_QK_DIM):
            cols = slice(h * RET_QK_DIM, (h + 1) * RET_QK_DIM)
            o_ref[:, cols] = _dot(x_ref[...], w_ref[:, cols]).astype(BF16)


def _ret_in_proj(x_bf, w_bf, layer, cos, sin, seq):
    m = x_bf.shape[0]
    bm = min(2048, seq)
    bn = 1024
    n_pos_blocks = seq // bm
    return pl.pallas_call(
        functools.partial(_ret_in_kernel, bn=bn),
        grid=(m // bm, RET_IN // bn),
        in_specs=[
            pl.BlockSpec((bm, D_MODEL), lambda i, j: (i, 0)),
            pl.BlockSpec((None, D_MODEL, bn), lambda i, j: (layer, 0, j)),
            pl.BlockSpec((bm, RET_QK_DIM // 2), lambda i, j: (i % n_pos_blocks, 0)),
            pl.BlockSpec((bm, RET_QK_DIM // 2), lambda i, j: (i % n_pos_blocks, 0)),
        ],
        out_specs=pl.BlockSpec((bm, bn), lambda i, j: (i, j)),
        out_shape=jax.ShapeDtypeStruct((m, RET_IN), BF16),
        compiler_params=_params(("parallel", "arbitrary")),
        name="ret_in_proj",
    )(x_bf, w_bf, cos, sin)


def _log_sigmoid(x):
    return jnp.minimum(x, 0.0) - jnp.log1p(jnp.exp(-jnp.abs(x)))


def _retention_kernel(q_ref, k_ref, v_ref, g_ref, df_ref, db_ref, o_ref,
                      state_ref, obuf_ref, dmat_ref, xi_ref, zeta_ref, *, chunk, n_sub):
    ps = pl.program_id(2)
    t = pl.program_id(3)
    nt = pl.num_programs(3)
    c = chunk

    @pl.when(t == 0)
    def _():
        state_ref[...] = jnp.zeros_like(state_ref)

    @pl.when((t == 0) & (ps == 0))
    def _():
        lf = _log_sigmoid(df_ref[...])
        lb = _log_sigmoid(db_ref[...])
        row = lax.broadcasted_iota(jnp.int32, (c, 128), 0).astype(F32)
        xi_ref[0] = jnp.exp(lb * (c - row))
        xi_ref[1] = jnp.exp(lf * (row + 1.0))
        zeta_ref[0] = jnp.exp(lb * row)
        zeta_ref[1] = jnp.exp(lf * (c - 1.0 - row))
        for jb in range(c // 128):
            col = lax.broadcasted_iota(jnp.int32, (c, 128), 1).astype(F32) + (128.0 * jb)
            rel = row - col
            dmat_ref[:, jb * 128:(jb + 1) * 128] = jnp.where(
                rel >= 0.0, jnp.exp(lf * jnp.maximum(rel, 0.0)), jnp.exp(lb * jnp.maximum(-rel, 0.0)))

    def tile128(tab, width):
        return jnp.concatenate([tab] * (width // 128), axis=1)

    def state_step(qc, kc, vc, direction):
        xi = xi_ref[direction]
        zeta = zeta_ref[direction]
        st = state_ref[...]
        inter = _dot(qc, st.astype(BF16)) * tile128(xi, RET_V_DIM)
        kz = (kc.astype(F32) * tile128(zeta, RET_QK_DIM)).astype(BF16)
        g_chunk = xi[c - 1:c, :] if direction == 1 else xi[0:1, :]
        state_ref[...] = st * tile128(g_chunk, RET_V_DIM) + _dot_tn(kz, vc)
        return inter

    @pl.when(ps == 0)
    def _():
        for s in reversed(range(n_sub)):
            rows = pl.ds(s * c, c)
            inter = state_step(q_ref[0, rows, :], k_ref[0, rows, :], v_ref[0, rows, :], 0)
            base = pl.multiple_of(((nt - 1 - t) * n_sub + s) * c, c)
            obuf_ref[pl.ds(base, c), :] = inter

    @pl.when(ps == 1)
    def _():
        for s in range(n_sub):
            rows = pl.ds(s * c, c)
            qc = q_ref[0, rows, :]
            kc = k_ref[0, rows, :]
            vc = v_ref[0, rows, :]
            sc = (_dot_nt(qc, kc) * dmat_ref[...]).astype(BF16)
            base = pl.multiple_of((t * n_sub + s) * c, c)
            o = _dot(sc, vc) + obuf_ref[pl.ds(base, c), :]
            o = o + state_step(qc, kc, vc, 1)
            mu = jnp.mean(o, axis=-1, keepdims=True)
            d = o - mu
            var = jnp.mean(d * d, axis=-1, keepdims=True)
            o = d * lax.rsqrt(var + GN_EPS)
            gate = g_ref[0, rows, :].astype(F32)
            o_ref[0, rows, :] = (gate * jax.nn.sigmoid(gate) * o).astype(BF16)


def _retention(p, decay_f, decay_b, layer, batch, seq):
    chunk = min(256, seq)
    tb = min(1024, seq)
    n_sub = tb // chunk
    nt = seq // tb
    p3 = p.reshape(batch, seq, RET_IN)
    qk_blocks = RET_QK_W // RET_QK_DIM
    v_first = 2 * RET_QK_W // RET_V_DIM
    g_first = v_first + RET_HEADS

    def pos(ps, t):
        return jnp.where(ps == 0, nt - 1 - t, t)

    return pl.pallas_call(
        functools.partial(_retention_kernel, chunk=chunk, n_sub=n_sub),
        grid=(batch, RET_HEADS, 2, nt),
        in_specs=[
            pl.BlockSpec((1, tb, RET_QK_DIM), lambda b, h, ps, t: (b, pos(ps, t), h)),
            pl.BlockSpec((1, tb, RET_QK_DIM), lambda b, h, ps, t: (b, pos(ps, t), qk_blocks + h)),
            pl.BlockSpec((1, tb, RET_V_DIM), lambda b, h, ps, t: (b, pos(ps, t), v_first + h)),
            pl.BlockSpec((1, tb, RET_V_DIM), lambda b, h, ps, t: (b, t * ps, g_first + h)),
            pl.BlockSpec((None, None, 1, 128), lambda b, h, ps, t: (layer, h, 0, 0)),
            pl.BlockSpec((None, None, 1, 128), lambda b, h, ps, t: (layer, h, 0, 0)),
        ],
        out_specs=pl.BlockSpec((1, tb, RET_V_DIM), lambda b, h, ps, t: (b, t * ps, h)),
        out_shape=jax.ShapeDtypeStruct((batch, seq, RET_V_W), BF16),
        scratch_shapes=[
            pltpu.VMEM((RET_QK_DIM, RET_V_DIM), F32),
            pltpu.VMEM((seq, RET_V_DIM), F32),
            pltpu.VMEM((chunk, chunk), F32),
            pltpu.VMEM((2, chunk, 128), F32),
            pltpu.VMEM((2, chunk, 128), F32),
        ],
        compiler_params=_params(("parallel", "parallel", "arbitrary", "arbitrary")),
        name="retention",
    )(p3, p3, p3, p3, decay_f, decay_b).reshape(batch * seq, RET_V_W)


def _mm_res_ln_kernel(a_ref, w_ref, x_ref, g_ref, b_ref, of_ref, ob_ref, *, ln_rows):
    kk = pl.program_id(1)
    last = pl.num_programs(1) - 1

    @pl.when(kk == 0)
    def _():
        of_ref[...] = _dot(a_ref[...], w_ref[...])

    @pl.when((kk > 0) & (kk < last))
    def _():
        of_ref[...] += _dot(a_ref[...], w_ref[...])

    @pl.when(kk == last)
    def _():
        of_ref[...] += _dot(a_ref[...], w_ref[...])
        for r in range(of_ref.shape[0] // ln_rows):
            rows = slice(r * ln_rows, (r + 1) * ln_rows)
            y = ALPHA * x_ref[rows, :] + of_ref[rows, :]
            mu = jnp.mean(y, axis=-1, keepdims=True)
            d = y - mu
            var = jnp.mean(d * d, axis=-1, keepdims=True)
            y = d * lax.rsqrt(var + LN_EPS) * g_ref[...] + b_ref[...]
            of_ref[rows, :] = y
            ob_ref[rows, :] = y.astype(BF16)


def _mm_res_ln(a_bf, w_bf, layer, x_f32, g, b):
    m, k = a_bf.shape
    bm = min(1024, m)
    bk = 512
    assert k // bk >= 2
    return pl.pallas_call(
        functools.partial(_mm_res_ln_kernel, ln_rows=min(256, bm)),
        grid=(m // bm, k // bk),
        in_specs=[
            pl.BlockSpec((bm, bk), lambda i, kk: (i, kk)),
            pl.BlockSpec((None, bk, D_MODEL), lambda i, kk: (layer, kk, 0)),
            pl.BlockSpec((bm, D_MODEL), lambda i, kk: (i, 0)),
            pl.BlockSpec((1, D_MODEL), lambda i, kk: (0, 0)),
            pl.BlockSpec((1, D_MODEL), lambda i, kk: (0, 0)),
        ],
        out_specs=[
            pl.BlockSpec((bm, D_MODEL), lambda i, kk: (i, 0)),
            pl.BlockSpec((bm, D_MODEL), lambda i, kk: (i, 0)),
        ],
        out_shape=[
            jax.ShapeDtypeStruct((m, D_MODEL), F32),
            jax.ShapeDtypeStruct((m, D_MODEL), BF16),
        ],
        compiler_params=_params(("parallel", "arbitrary")),
        name="mm_res_ln",
    )(a_bf, w_bf, x_f32, g.reshape(1, D_MODEL), b.reshape(1, D_MODEL))


def _rms(x, g):
    return x * lax.rsqrt(jnp.mean(x * x, axis=-1, keepdims=True) + RMS_EPS) * g


def _mla_in_kernel(x_ref, w_ref, qg_ref, kg_ref, tab_ref, cq_ref, ckv_ref, kr_ref):
    acc = _dot(x_ref[...], w_ref[...])
    cq_ref[...] = _rms(acc[:, :MLA_Q_RANK], qg_ref[...]).astype(BF16)
    ckv_ref[...] = _rms(acc[:, MLA_Q_RANK:MLA_Q_RANK + MLA_KV_RANK], kg_ref[...]).astype(BF16)
    part = acc[:, MLA_Q_RANK + MLA_KV_RANK:] * tab_ref[...]
    kr_ref[...] = (part + pltpu.roll(part, 64, axis=1)).astype(BF16)


def _mla_in_proj(x_bf, w_bf, layer, q_norm, kv_norm, ktab, seq):
    m = x_bf.shape[0]
    bm = min(512, seq)
    n_pos_blocks = seq // bm
    return pl.pallas_call(
        _mla_in_kernel,
        grid=(m // bm,),
        in_specs=[
            pl.BlockSpec((bm, D_MODEL), lambda i: (i, 0)),
            pl.BlockSpec((None, D_MODEL, MLA_IN_W), lambda i: (layer, 0, 0)),
            pl.BlockSpec((1, MLA_Q_RANK), lambda i: (0, 0)),
            pl.BlockSpec((1, MLA_KV_RANK), lambda i: (0, 0)),
            pl.BlockSpec((bm, 128), lambda i: (i % n_pos_blocks, 0)),
        ],
        out_specs=[
            pl.BlockSpec((bm, MLA_Q_RANK), lambda i: (i, 0)),
            pl.BlockSpec((bm, MLA_KV_RANK), lambda i: (i, 0)),
            pl.BlockSpec((bm, 128), lambda i: (i, 0)),
        ],
        out_shape=[
            jax.ShapeDtypeStruct((m, MLA_Q_RANK), BF16),
            jax.ShapeDtypeStruct((m, MLA_KV_RANK), BF16),
            jax.ShapeDtypeStruct((m, 128), BF16),
        ],
        compiler_params=_params(("parallel",)),
        name="mla_in_proj",
    )(x_bf, w_bf, q_norm.reshape(1, -1), kv_norm.reshape(1, -1), ktab)


def _mla_q_kernel(c_ref, w_ref, tab_ref, o_ref, *, heads_per_tile):
    acc = _dot(c_ref[...], w_ref[...])
    tab = tab_ref[...]
    for h in range(heads_per_tile):
        lo = h * MLA_HEAD_W
        o_ref[:, lo:lo + MLA_HEAD_W] = (acc[:, lo:lo + MLA_HEAD_W] * tab).astype(BF16)


def _mla_q_proj(cq, w_bf, layer, qtab, seq):
    m = cq.shape[0]
    bm = min(1024, seq)
    bn = 1024
    n_pos_blocks = seq // bm
    n_out = MLA_HEADS * MLA_HEAD_W
    return pl.pallas_call(
        functools.partial(_mla_q_kernel, heads_per_tile=bn // MLA_HEAD_W),
        grid=(m // bm, n_out // bn),
        in_specs=[
            pl.BlockSpec((bm, MLA_Q_RANK), lambda i, j: (i, 0)),
            pl.BlockSpec((None, MLA_Q_RANK, bn), lambda i, j: (layer, 0, j)),
            pl.BlockSpec((bm, MLA_HEAD_W), lambda i, j: (i % n_pos_blocks, 0)),
        ],
        out_specs=pl.BlockSpec((bm, bn), lambda i, j: (i, j)),
        out_shape=jax.ShapeDtypeStruct((m, n_out), BF16),
        compiler_params=_params(("parallel", "arbitrary")),
        name="mla_q_proj",
    )(cq, w_bf, qtab)


def _mla_kv_kernel(c_ref, w_ref, kr_ref, k_ref, v_ref):
    acc = _dot(c_ref[...], w_ref[...])
    kr = kr_ref[...]
    ones = jnp.ones((acc.shape[0], MLA_V_W - MLA_V), BF16)
    v_first = MLA_HEADS * MLA_NOPE
    for h in range(MLA_HEADS):
        lo = h * MLA_HEAD_W
        k_ref[:, lo:lo + MLA_NOPE] = acc[:, h * MLA_NOPE:(h + 1) * MLA_NOPE].astype(BF16)
        k_ref[:, lo + MLA_NOPE:lo + MLA_HEAD_W] = kr
        lo = h * MLA_V_W
        v_ref[:, lo:lo + MLA_V] = acc[:, v_first + h * MLA_V:v_first + (h + 1) * MLA_V].astype(BF16)
        v_ref[:, lo + MLA_V:lo + MLA_V_W] = ones


def _mla_kv_proj(ckv, w_bf, layer, kr):
    m = ckv.shape[0]
    bm = min(512, m)
    return pl.pallas_call(
        _mla_kv_kernel,
        grid=(m // bm,),
        in_specs=[
            pl.BlockSpec((bm, MLA_KV_RANK), lambda i: (i, 0)),
            pl.BlockSpec((None, MLA_KV_RANK, MLA_HEADS * (MLA_NOPE + MLA_V)), lambda i: (layer, 0, 0)),
            pl.BlockSpec((bm, 128), lambda i: (i, 0)),
        ],
        out_specs=[
            pl.BlockSpec((bm, MLA_HEADS * MLA_HEAD_W), lambda i: (i, 0)),
            pl.BlockSpec((bm, MLA_HEADS * MLA_V_W), lambda i: (i, 0)),
        ],
        out_shape=[
            jax.ShapeDtypeStruct((m, MLA_HEADS * MLA_HEAD_W), BF16),
            jax.ShapeDtypeStruct((m, MLA_HEADS * MLA_V_W), BF16),
        ],
        compiler_params=_params(("parallel",)),
        name="mla_kv_proj",
    )(ckv, w_bf, kr)


def _attn_kernel(q_ref, k_ref, v_ref, o_ref, *, kc, group):
    q = q_ref[0]
    bq = q.shape[0]
    n_groups = k_ref.shape[1] // (kc * group)

    def kv_rows(g, c):
        return pl.ds(pl.multiple_of((g * group + c) * kc, kc), kc)

    def body(g, carry):
        m, acc = carry
        s_next = _dot_nt(q, k_ref[0, kv_rows(g, 0), :])
        for c in range(group):
            s = s_next
            if c + 1 < group:
                s_next = _dot_nt(q, k_ref[0, kv_rows(g, c + 1), :])
            m_new = jnp.maximum(m, jnp.max(s, axis=-1, keepdims=True))
            alpha = jnp.exp2(m - m_new)
            p = jnp.exp2(s - m_new).astype(BF16)
            acc = alpha * acc + _dot(p, v_ref[0, kv_rows(g, c), :])
            m = m_new
        return m, acc

    m0 = jnp.full((bq, 1), -jnp.inf, F32)
    a0 = jnp.zeros((bq, MLA_V_W), F32)
    _, acc = lax.fori_loop(0, n_groups, body, (m0, a0))
    o_ref[0] = (acc[:, :MLA_V] / acc[:, MLA_V:]).astype(BF16)


def _attention(q, k, v, batch, seq):
    bq = min(1024, seq)
    kc = min(512, seq)
    group = min(16, seq // kc)
    q3 = q.reshape(batch, seq, MLA_HEADS * MLA_HEAD_W)
    k3 = k.reshape(batch, seq, MLA_HEADS * MLA_HEAD_W)
    v3 = v.reshape(batch, seq, MLA_HEADS * MLA_V_W)
    return pl.pallas_call(
        functools.partial(_attn_kernel, kc=kc, group=group),
        grid=(batch, MLA_HEADS, seq // bq),
        in_specs=[
            pl.BlockSpec((1, bq, MLA_HEAD_W), lambda b, h, i: (b, i, h)),
            pl.BlockSpec((1, seq, MLA_HEAD_W), lambda b, h, i: (b, 0, h)),
            pl.BlockSpec((1, seq, MLA_V_W), lambda b, h, i: (b, 0, h)),
        ],
        out_specs=pl.BlockSpec((1, bq, MLA_V), lambda b, h, i: (b, i, h)),
        out_shape=jax.ShapeDtypeStruct((batch, seq, MLA_HEADS * MLA_V), BF16),
        compiler_params=_params(("parallel", "parallel", "arbitrary")),
        name="attention",
    )(q3, k3, v3).reshape(batch * seq, MLA_HEADS * MLA_V)


def _ffn_halo_kernel(x_ref, w_ref, o_ref):
    o_ref[...] = _dot(x_ref[...], w_ref[...])


def _ffn_halo(x_rows, w_up_bf, layer):
    r = x_rows.shape[0]
    bn = 512
    g_first = FFN_DIM // bn
    return pl.pallas_call(
        _ffn_halo_kernel,
        grid=(FFN_DIM // bn,),
        in_specs=[
            pl.BlockSpec((r, D_MODEL), lambda j: (0, 0)),
            pl.BlockSpec((None, D_MODEL, bn), lambda j: (layer, 0, g_first + j)),
        ],
        out_specs=pl.BlockSpec((r, bn), lambda j: (0, j)),
        out_shape=jax.ShapeDtypeStruct((r, FFN_DIM), F32),
        compiler_params=_params(("parallel",)),
        name="ffn_halo",
    )(x_rows, w_up_bf)


def _ffn_up_kernel(x_ref, wu_ref, wg_ref, cw_ref, cb_ref, hp_ref, hn_ref, o_ref):
    x = x_ref[...]
    bm = x.shape[0]
    for c in range(o_ref.shape[1] // FFN_SLAB):
        cols = slice(c * FFN_SLAB, (c + 1) * FFN_SLAB)
        u = _dot(x, wu_ref[:, cols])
        g = _dot(x, wg_ref[:, cols])
        row = lax.broadcasted_iota(jnp.int32, g.shape, 0)
        g_prev = jnp.where(row == 0, hp_ref[0, :, cols], pltpu.roll(g, 1, axis=0))
        g_next = jnp.where(row == bm - 1, hn_ref[0, :, cols], pltpu.roll(g, bm - 1, axis=0))
        gc = (g_prev * cw_ref[0:1, cols] + g * cw_ref[1:2, cols] + g_next * cw_ref[2:3, cols]
              + cb_ref[:, cols])
        o_ref[:, cols] = (gc * jax.nn.sigmoid(gc) * u).astype(BF16)


def _ffn_up(x_bf, w_up_bf, layer, conv_w, conv_b, seq):
    m = x_bf.shape[0]
    bm = min(1024, seq)
    bn = 512
    nmb = m // bm
    g_first = FFN_DIM // bn
    xr = x_bf.reshape(nmb, bm, D_MODEL)
    tile_start = np.arange(nmb) * bm
    has_prev = jnp.asarray((tile_start % seq) != 0)[:, None]
    has_next = jnp.asarray(((tile_start + bm) % seq) != 0)[:, None]
    x_prev = jnp.where(has_prev, jnp.roll(xr[:, bm - 1, :], 1, axis=0), 0).astype(BF16)
    x_next = jnp.where(has_next, jnp.roll(xr[:, 0, :], -1, axis=0), 0).astype(BF16)
    pad = (-2 * nmb) % 16
    rows = jnp.concatenate([x_prev, x_next, jnp.zeros((pad, D_MODEL), BF16)], axis=0)
    halo = _ffn_halo(rows, w_up_bf, layer)
    h_prev = halo[:nmb].reshape(nmb, 1, FFN_DIM)
    h_next = halo[nmb:2 * nmb].reshape(nmb, 1, FFN_DIM)
    return pl.pallas_call(
        _ffn_up_kernel,
        grid=(nmb, FFN_DIM // bn),
        in_specs=[
            pl.BlockSpec((bm, D_MODEL), lambda i, j: (i, 0)),
            pl.BlockSpec((None, D_MODEL, bn), lambda i, j: (layer, 0, j)),
            pl.BlockSpec((None, D_MODEL, bn), lambda i, j: (layer, 0, g_first + j)),
            pl.BlockSpec((3, bn), lambda i, j: (0, j)),
            pl.BlockSpec((1, bn), lambda i, j: (0, j)),
            pl.BlockSpec((1, 1, bn), lambda i, j: (i, 0, j)),
            pl.BlockSpec((1, 1, bn), lambda i, j: (i, 0, j)),
        ],
        out_specs=pl.BlockSpec((bm, bn), lambda i, j: (i, j)),
        out_shape=jax.ShapeDtypeStruct((m, FFN_DIM), BF16),
        compiler_params=_params(("parallel", "arbitrary")),
        name="ffn_up",
    )(x_bf, w_up_bf, w_up_bf, conv_w, conv_b.reshape(1, FFN_DIM), h_prev, h_next)


def _rope_tables(seq, dim, base):
    inv = 1.0 / (base ** (jnp.arange(0, dim // 2, dtype=F32) * (2.0 / dim)))
    ang = jnp.arange(seq, dtype=F32)[:, None] * inv[None, :]
    return jnp.cos(ang), jnp.sin(ang)


def _rotary_quad(r):
    half = MLA_ROPE // 2
    x1, x2 = r[..., :half], r[..., half:]
    return jnp.concatenate([x1, x2, x2, x1], axis=-1)


def _mla_weight_layouts(w_in, w_uq, w_ukv):
    n_layers = w_in.shape[0]
    lat = MLA_Q_RANK + MLA_KV_RANK
    w_in = jnp.concatenate([w_in[..., :lat], _rotary_quad(w_in[..., lat:])], axis=-1)
    q = w_uq.reshape(n_layers, MLA_Q_RANK, MLA_HEADS, MLA_NOPE + MLA_ROPE)
    q = jnp.concatenate([q[..., :MLA_NOPE], _rotary_quad(q[..., MLA_NOPE:])], axis=-1)
    w_uq = q.reshape(n_layers, MLA_Q_RANK, MLA_HEADS * MLA_HEAD_W)
    kv = w_ukv.reshape(n_layers, MLA_KV_RANK, MLA_HEADS, MLA_NOPE + MLA_V)
    w_ukv = jnp.concatenate([kv[..., :MLA_NOPE].reshape(n_layers, MLA_KV_RANK, MLA_HEADS * MLA_NOPE),
                             kv[..., MLA_NOPE:].reshape(n_layers, MLA_KV_RANK, MLA_HEADS * MLA_V)],
                            axis=-1)
    return w_in, w_uq, w_ukv


def _trunk(x, seq, w):
    batch = x.shape[0]
    m = batch * seq
    x_f = x.reshape(m, D_MODEL)
    x_b = x_f.astype(BF16)

    ret_cos, ret_sin = _rope_tables(seq, RET_QK_DIM, RET_ROPE_BASE)
    cos, sin = _rope_tables(seq, MLA_ROPE, MLA_ROPE_BASE)
    rot = jnp.concatenate([cos, cos, -sin, sin], axis=1)
    score_scale = (MLA_NOPE + MLA_ROPE) ** -0.5 * LOG2_E
    qtab = jnp.concatenate([jnp.ones((seq, MLA_NOPE), F32), rot], axis=1) * score_scale

    for i in range(DEPTH):
        j = i // 2
        if i % 2 == 0:
            p = _ret_in_proj(x_b, w["ret_w_in"], j, ret_cos, ret_sin, seq)
            mix = _retention(p, w["ret_decay_fwd"], w["ret_decay_bwd"], j, batch, seq)
            x_f, x_b = _mm_res_ln(mix, w["ret_w_out"], j, x_f, w["ln1_g"][i], w["ln1_b"][i])
        else:
            cq, ckv, kr = _mla_in_proj(x_b, w["mla_w_in"], j, w["mla_q_norm"][j],
                                       w["mla_kv_norm"][j], rot, seq)
            q = _mla_q_proj(cq, w["mla_w_uq"], j, qtab, seq)
            k, v = _mla_kv_proj(ckv, w["mla_w_ukv"], j, kr)
            mix = _attention(q, k, v, batch, seq)
            x_f, x_b = _mm_res_ln(mix, w["mla_w_out"], j, x_f, w["ln1_g"][i], w["ln1_b"][i])
        a = _ffn_up(x_b, w["ffn_w_up"], i, w["ffn_conv_w"][i], w["ffn_conv_b"][i], seq)
        x_f, x_b = _mm_res_ln(a, w["ffn_w_down"], i, x_f, w["ln2_g"][i], w["ln2_b"][i])
    return x_f.reshape(batch, seq, D_MODEL)


def _prepare_weights(ret_w_in, ret_decay_fwd, ret_decay_bwd, ret_w_out,
                     mla_w_in, mla_q_norm, mla_kv_norm, mla_w_uq, mla_w_ukv, mla_w_out,
                     ln1_g, ln1_b, ln2_g, ln2_b, ffn_w_up, ffn_conv_w, ffn_conv_b, ffn_w_down):
    mla_w_in, mla_w_uq, mla_w_ukv = _mla_weight_layouts(mla_w_in, mla_w_uq, mla_w_ukv)

    def lanes(d):
        return jnp.broadcast_to(d.astype(F32)[:, :, None, None], d.shape + (1, 128))

    return {
        "ret_w_in": ret_w_in.astype(BF16),
        "ret_decay_fwd": lanes(ret_decay_fwd),
        "ret_decay_bwd": lanes(ret_decay_bwd),
        "ret_w_out": ret_w_out.astype(BF16),
        "mla_w_in": mla_w_in.astype(BF16),
        "mla_q_norm": mla_q_norm.astype(F32),
        "mla_kv_norm": mla_kv_norm.astype(F32),
        "mla_w_uq": mla_w_uq.astype(BF16),
        "mla_w_ukv": mla_w_ukv.astype(BF16),
        "mla_w_out": mla_w_out.astype(BF16),
        "ln1_g": ln1_g.astype(F32), "ln1_b": ln1_b.astype(F32),
        "ln2_g": ln2_g.astype(F32), "ln2_b": ln2_b.astype(F32),
        "ffn_w_up": ffn_w_up.astype(BF16),
        "ffn_conv_w": ffn_conv_w.astype(F32),
        "ffn_conv_b": ffn_conv_b.astype(F32),
        "ffn_w_down": ffn_w_down.astype(BF16),
    }


def kernel(x_prompt, x_sample, ret_w_in, ret_decay_fwd, ret_decay_bwd, ret_w_out, mla_w_in, mla_q_norm, mla_kv_norm, mla_w_uq, mla_w_ukv, mla_w_out, ln1_g, ln1_b, ln2_g, ln2_b, ffn_w_up, ffn_conv_w, ffn_conv_b, ffn_w_down):
    w = _prepare_weights(ret_w_in, ret_decay_fwd, ret_decay_bwd, ret_w_out,
                         mla_w_in, mla_q_norm, mla_kv_norm, mla_w_uq, mla_w_ukv, mla_w_out,
                         ln1_g, ln1_b, ln2_g, ln2_b, ffn_w_up, ffn_conv_w, ffn_conv_b, ffn_w_down)
    y_prompt = _trunk(x_prompt, x_prompt.shape[1], w)
    y_sample = _trunk(x_sample, x_sample.shape[1], w)
    return (y_prompt, y_sample)
```

```python
import functools
import math

import jax
import jax.numpy as jnp
import numpy as np
from jax import lax
from jax.experimental import pallas as pl
from jax.experimental.pallas import tpu as pltpu

F32 = jnp.float32
BF16 = jnp.bfloat16

D_MODEL = 2048
DEPTH = 4
ALPHA = (2.0 * DEPTH) ** 0.25

RET_HEADS = 8
RET_QK_DIM = 256
RET_V_DIM = 512
RET_QK_W = RET_HEADS * RET_QK_DIM
RET_V_W = RET_HEADS * RET_V_DIM
RET_IN = 2 * RET_QK_W + 2 * RET_V_W
RET_ROPE_BASE = 10000.0

MLA_HEADS = 16
MLA_Q_RANK = 1536
MLA_KV_RANK = 512
MLA_NOPE = 128
MLA_ROPE = 64
MLA_V = 128
MLA_ROPE_BASE = 10000.0
MLA_HEAD_W = 256
MLA_IN_W = MLA_Q_RANK + MLA_KV_RANK + 128
MLA_V_W = 256

FFN_DIM = 5632
FFN_SLAB = 256

LN_EPS = 1e-5
RMS_EPS = 1e-6
GN_EPS = 1e-6

VMEM_LIMIT_BYTES = 56 * 1024 * 1024
LOG2_E = 1.4426950408889634


def _params(semantics):
    return pltpu.CompilerParams(dimension_semantics=semantics,
                                vmem_limit_bytes=VMEM_LIMIT_BYTES)


def _dot(a, b):
    return jnp.dot(a, b, preferred_element_type=F32)


def _dot_nt(a, b):
    return lax.dot_general(a, b, (((1,), (1,)), ((), ())), preferred_element_type=F32)


def _dot_tn(a, b):
    return lax.dot_general(a, b, (((0,), (0,)), ((), ())), preferred_element_type=F32)


def _ret_in_kernel(x_ref, w_ref, cos_ref, sin_ref, o_ref, *, bn):
    j = pl.program_id(1)
    n_qk_tiles = 2 * RET_QK_W // bn

    @pl.when(j < n_qk_tiles)
    def _():
        cos = cos_ref[...]
        sin = sin_ref[...]
        scale = jnp.where(j >= n_qk_tiles // 2, RET_QK_DIM ** -0.5, 1.0).astype(F32)
        half = RET_QK_DIM // 2
        for h in range(bn // RET_QK_DIM):
            lo = h * RET_QK_DIM
            acc = _dot(x_ref[...], w_ref[:, lo:lo + RET_QK_DIM])
            x1 = acc[:, :half]
            x2 = acc[:, half:]
            o_ref[:, lo:lo + half] = ((x1 * cos - x2 * sin) * scale).astype(BF16)
            o_ref[:, lo + half:lo + 2 * half] = ((x1 * sin + x2 * cos) * scale).astype(BF16)

    n_v_tiles = RET_V_W // bn

    @pl.when((j >= n_qk_tiles) & (j < n_qk_tiles + n_v_tiles))
    def _():
        for h in range(bn // RET_QK_DIM):
            cols = slice(h * RET_QK_DIM, (h + 1) * RET_QK_DIM)
            o_ref[:, cols] = _dot(x_ref[...], w_ref[:, cols]).astype(BF16)

    @pl.when(j >= n_qk_tiles + n_v_tiles)
    def _():
        for h in range(bn // RET_QK_DIM):
            cols = slice(h * RET_QK_DIM, (h + 1) * RET_QK_DIM)
            gate = _dot(x_ref[...], w_ref[:, cols])
            o_ref[:, cols] = (gate * jax.nn.sigmoid(gate)).astype(BF16)


def _ret_in_proj(x_bf, w_bf, layer, cos, sin, seq):
    m = x_bf.shape[0]
    bm = min(2048, seq)
    bn = 1024
    n_pos_blocks = seq // bm
    return pl.pallas_call(
        functools.partial(_ret_in_kernel, bn=bn),
        grid=(m // bm, RET_IN // bn),
        in_specs=[
            pl.BlockSpec((bm, D_MODEL), lambda i, j: (i, 0)),
            pl.BlockSpec((None, D_MODEL, bn), lambda i, j: (layer, 0, j)),
            pl.BlockSpec((bm, RET_QK_DIM // 2), lambda i, j: (i % n_pos_blocks, 0)),
            pl.BlockSpec((bm, RET_QK_DIM // 2), lambda i, j: (i % n_pos_blocks, 0)),
        ],
        out_specs=pl.BlockSpec((bm, bn), lambda i, j: (i, j)),
        out_shape=jax.ShapeDtypeStruct((m, RET_IN), BF16),
        compiler_params=_params(("parallel", "arbitrary")),
        name="ret_in_proj",
    )(x_bf, w_bf, cos, sin)


def _log_sigmoid(x):
    return jnp.minimum(x, 0.0) - jnp.log1p(jnp.exp(-jnp.abs(x)))


def _retention_kernel(q_ref, k_ref, v_ref, g_ref, df_ref, db_ref, o_ref,
                      state_ref, obuf_ref, dmat_ref, xi_ref, zeta_ref, *, chunk, n_sub):
    ps = pl.program_id(2)
    t = pl.program_id(3)
    nt = pl.num_programs(3)
    c = chunk

    @pl.when(t == 0)
    def _():
        state_ref[...] = jnp.zeros_like(state_ref)

    @pl.when((t == 0) & (ps == 0))
    def _():
        lf = _log_sigmoid(df_ref[...])
        lb = _log_sigmoid(db_ref[...])
        row = lax.broadcasted_iota(jnp.int32, (c, 128), 0).astype(F32)
        xi_ref[0] = jnp.exp(lb * (c - row))
        xi_ref[1] = jnp.exp(lf * (row + 1.0))
        zeta_ref[0] = jnp.exp(lb * row)
        zeta_ref[1] = jnp.exp(lf * (c - 1.0 - row))
        for jb in range(c // 128):
            col = lax.broadcasted_iota(jnp.int32, (c, 128), 1).astype(F32) + (128.0 * jb)
            rel = row - col
            dmat_ref[:, jb * 128:(jb + 1) * 128] = jnp.where(
                rel >= 0.0, jnp.exp(lf * jnp.maximum(rel, 0.0)), jnp.exp(lb * jnp.maximum(-rel, 0.0)))

    def tile128(tab, width):
        return jnp.concatenate([tab] * (width // 128), axis=1)

    def state_step(qc, kc, vc, direction):
        xi = xi_ref[direction]
        zeta = zeta_ref[direction]
        st = state_ref[...]
        inter = _dot(qc, st.astype(BF16)) * tile128(xi, RET_V_DIM)
        kz = (kc.astype(F32) * tile128(zeta, RET_QK_DIM)).astype(BF16)
        g_chunk = xi[c - 1:c, :] if direction == 1 else xi[0:1, :]
        state_ref[...] = st * tile128(g_chunk, RET_V_DIM) + _dot_tn(kz, vc)
        return inter

    @pl.when(ps == 0)
    def _():
        for s in reversed(range(n_sub)):
            rows = pl.ds(s * c, c)
            inter = state_step(q_ref[0, rows, :], k_ref[0, rows, :], v_ref[0, rows, :], 0)
            base = pl.multiple_of(((nt - 1 - t) * n_sub + s) * c, c)
            obuf_ref[pl.ds(base, c), :] = inter

    @pl.when(ps == 1)
    def _():
        for s in range(n_sub):
            rows = pl.ds(s * c, c)
            qc = q_ref[0, rows, :]
            kc = k_ref[0, rows, :]
            vc = v_ref[0, rows, :]
            sc = (_dot_nt(qc, kc) * dmat_ref[...]).astype(BF16)
            base = pl.multiple_of((t * n_sub + s) * c, c)
            o = _dot(sc, vc) + obuf_ref[pl.ds(base, c), :]
            o = o + state_step(qc, kc, vc, 1)
            mu = jnp.mean(o, axis=-1, keepdims=True)
            d = o - mu
            var = jnp.mean(d * d, axis=-1, keepdims=True)
            o = d * lax.rsqrt(var + GN_EPS)
            o_ref[0, rows, :] = (g_ref[0, rows, :].astype(F32) * o).astype(BF16)


def _retention(p, decay_f, decay_b, layer, batch, seq):
    chunk = min(256, seq)
    tb = min(2048, seq)
    n_sub = tb // chunk
    nt = seq // tb
    p3 = p.reshape(batch, seq, RET_IN)
    qk_blocks = RET_QK_W // RET_QK_DIM
    v_first = 2 * RET_QK_W // RET_V_DIM
    g_first = v_first + RET_HEADS

    def pos(ps, t):
        return jnp.where(ps == 0, nt - 1 - t, t)

    return pl.pallas_call(
        functools.partial(_retention_kernel, chunk=chunk, n_sub=n_sub),
        grid=(batch, RET_HEADS, 2, nt),
        in_specs=[
            pl.BlockSpec((1, tb, RET_QK_DIM), lambda b, h, ps, t: (b, pos(ps, t), h)),
            pl.BlockSpec((1, tb, RET_QK_DIM), lambda b, h, ps, t: (b, pos(ps, t), qk_blocks + h)),
            pl.BlockSpec((1, tb, RET_V_DIM), lambda b, h, ps, t: (b, pos(ps, t), v_first + h)),
            pl.BlockSpec((1, tb, RET_V_DIM), lambda b, h, ps, t: (b, t * ps, g_first + h)),
            pl.BlockSpec((None, None, 1, 128), lambda b, h, ps, t: (layer, h, 0, 0)),
            pl.BlockSpec((None, None, 1, 128), lambda b, h, ps, t: (layer, h, 0, 0)),
        ],
        out_specs=pl.BlockSpec((1, tb, RET_V_DIM), lambda b, h, ps, t: (b, t * ps, h)),
        out_shape=jax.ShapeDtypeStruct((batch, seq, RET_V_W), BF16),
        scratch_shapes=[
            pltpu.VMEM((RET_QK_DIM, RET_V_DIM), F32),
            pltpu.VMEM((seq, RET_V_DIM), F32),
            pltpu.VMEM((chunk, chunk), F32),
            pltpu.VMEM((2, chunk, 128), F32),
            pltpu.VMEM((2, chunk, 128), F32),
        ],
        compiler_params=_params(("parallel", "parallel", "arbitrary", "arbitrary")),
        name="retention",
    )(p3, p3, p3, p3, decay_f, decay_b).reshape(batch * seq, RET_V_W)


def _residual_layer_norm(x_ref, g_ref, b_ref, of_ref, ob_ref, ln_rows):
    for r in range(of_ref.shape[0] // ln_rows):
        rows = slice(r * ln_rows, (r + 1) * ln_rows)
        y = ALPHA * x_ref[rows, :] + of_ref[rows, :]
        mu = jnp.mean(y, axis=-1, keepdims=True)
        d = y - mu
        var = jnp.mean(d * d, axis=-1, keepdims=True)
        y = d * lax.rsqrt(var + LN_EPS) * g_ref[...] + b_ref[...]
        of_ref[rows, :] = y
        ob_ref[rows, :] = y.astype(BF16)


def _mm_res_ln_kernel(a_ref, w_ref, x_ref, g_ref, b_ref, of_ref, ob_ref, *, ln_rows, n_k):
    if n_k == 1:
        of_ref[...] = _dot(a_ref[...], w_ref[...])
        _residual_layer_norm(x_ref, g_ref, b_ref, of_ref, ob_ref, ln_rows)
        return

    kk = pl.program_id(1)

    @pl.when(kk == 0)
    def _():
        of_ref[...] = _dot(a_ref[...], w_ref[...])

    @pl.when((kk > 0) & (kk < n_k - 1))
    def _():
        of_ref[...] += _dot(a_ref[...], w_ref[...])

    @pl.when(kk == n_k - 1)
    def _():
        of_ref[...] += _dot(a_ref[...], w_ref[...])
        _residual_layer_norm(x_ref, g_ref, b_ref, of_ref, ob_ref, ln_rows)


RESIDENT_WEIGHT_BYTES = 16 * 1024 * 1024


def _mm_res_ln(a_bf, w_bf, layer, x_f32, g, b):
    m, k = a_bf.shape
    if k * D_MODEL * 2 <= RESIDENT_WEIGHT_BYTES:
        bm, bk = min(512, m), k
        w_spec = pl.BlockSpec((None, bk, D_MODEL), lambda i, kk: (layer, 0, 0),
                              pipeline_mode=pl.Buffered(1))
    else:
        bm, bk = min(1024, m), 512
        w_spec = pl.BlockSpec((None, bk, D_MODEL), lambda i, kk: (layer, kk, 0))
    n_k = k // bk
    return pl.pallas_call(
        functools.partial(_mm_res_ln_kernel, ln_rows=min(256, bm), n_k=n_k),
        grid=(m // bm, n_k),
        in_specs=[
            pl.BlockSpec((bm, bk), lambda i, kk: (i, kk)),
            w_spec,
            pl.BlockSpec((bm, D_MODEL), lambda i, kk: (i, 0)),
            pl.BlockSpec((1, D_MODEL), lambda i, kk: (0, 0)),
            pl.BlockSpec((1, D_MODEL), lambda i, kk: (0, 0)),
        ],
        out_specs=[
            pl.BlockSpec((bm, D_MODEL), lambda i, kk: (i, 0)),
            pl.BlockSpec((bm, D_MODEL), lambda i, kk: (i, 0)),
        ],
        out_shape=[
            jax.ShapeDtypeStruct((m, D_MODEL), F32),
            jax.ShapeDtypeStruct((m, D_MODEL), BF16),
        ],
        compiler_params=_params(("parallel", "arbitrary")),
        name="mm_res_ln",
    )(a_bf, w_bf, x_f32, g.reshape(1, D_MODEL), b.reshape(1, D_MODEL))


def _rms(x, g):
    return x * lax.rsqrt(jnp.mean(x * x, axis=-1, keepdims=True) + RMS_EPS) * g


def _mla_in_kernel(x_ref, w_ref, qg_ref, kg_ref, tab_ref, cq_ref, ckv_ref, kr_ref):
    acc = _dot(x_ref[...], w_ref[...])
    cq_ref[...] = _rms(acc[:, :MLA_Q_RANK], qg_ref[...]).astype(BF16)
    ckv_ref[...] = _rms(acc[:, MLA_Q_RANK:MLA_Q_RANK + MLA_KV_RANK], kg_ref[...]).astype(BF16)
    part = acc[:, MLA_Q_RANK + MLA_KV_RANK:] * tab_ref[...]
    kr_ref[...] = (part + pltpu.roll(part, 64, axis=1)).astype(BF16)


def _mla_in_proj(x_bf, w_bf, layer, q_norm, kv_norm, ktab, seq):
    m = x_bf.shape[0]
    bm = min(512, seq)
    n_pos_blocks = seq // bm
    return pl.pallas_call(
        _mla_in_kernel,
        grid=(m // bm,),
        in_specs=[
            pl.BlockSpec((bm, D_MODEL), lambda i: (i, 0)),
            pl.BlockSpec((None, D_MODEL, MLA_IN_W), lambda i: (layer, 0, 0)),
            pl.BlockSpec((1, MLA_Q_RANK), lambda i: (0, 0)),
            pl.BlockSpec((1, MLA_KV_RANK), lambda i: (0, 0)),
            pl.BlockSpec((bm, 128), lambda i: (i % n_pos_blocks, 0)),
        ],
        out_specs=[
            pl.BlockSpec((bm, MLA_Q_RANK), lambda i: (i, 0)),
            pl.BlockSpec((bm, MLA_KV_RANK), lambda i: (i, 0)),
            pl.BlockSpec((bm, 128), lambda i: (i, 0)),
        ],
        out_shape=[
            jax.ShapeDtypeStruct((m, MLA_Q_RANK), BF16),
            jax.ShapeDtypeStruct((m, MLA_KV_RANK), BF16),
            jax.ShapeDtypeStruct((m, 128), BF16),
        ],
        compiler_params=_params(("parallel",)),
        name="mla_in_proj",
    )(x_bf, w_bf, q_norm.reshape(1, -1), kv_norm.reshape(1, -1), ktab)


def _mla_q_kernel(c_ref, w_ref, tab_ref, o_ref, *, heads_per_tile):
    acc = _dot(c_ref[...], w_ref[...])
    tab = tab_ref[...]
    for h in range(heads_per_tile):
        lo = h * MLA_HEAD_W
        o_ref[:, lo:lo + MLA_HEAD_W] = (acc[:, lo:lo + MLA_HEAD_W] * tab).astype(BF16)


def _mla_q_proj(cq, w_bf, layer, qtab, seq):
    m = cq.shape[0]
    bm = min(1024, seq)
    bn = 1024
    n_pos_blocks = seq // bm
    n_out = MLA_HEADS * MLA_HEAD_W
    return pl.pallas_call(
        functools.partial(_mla_q_kernel, heads_per_tile=bn // MLA_HEAD_W),
        grid=(m // bm, n_out // bn),
        in_specs=[
            pl.BlockSpec((bm, MLA_Q_RANK), lambda i, j: (i, 0)),
            pl.BlockSpec((None, MLA_Q_RANK, bn), lambda i, j: (layer, 0, j)),
            pl.BlockSpec((bm, MLA_HEAD_W), lambda i, j: (i % n_pos_blocks, 0)),
        ],
        out_specs=pl.BlockSpec((bm, bn), lambda i, j: (i, j)),
        out_shape=jax.ShapeDtypeStruct((m, n_out), BF16),
        compiler_params=_params(("parallel", "arbitrary")),
        name="mla_q_proj",
    )(cq, w_bf, qtab)


def _mla_kv_kernel(c_ref, w_ref, kr_ref, k_ref, v_ref):
    acc = _dot(c_ref[...], w_ref[...])
    kr = kr_ref[...]
    ones = jnp.ones((acc.shape[0], MLA_V_W - MLA_V), BF16)
    v_first = MLA_HEADS * MLA_NOPE
    for h in range(MLA_HEADS):
        lo = h * MLA_HEAD_W
        k_ref[:, lo:lo + MLA_NOPE] = acc[:, h * MLA_NOPE:(h + 1) * MLA_NOPE].astype(BF16)
        k_ref[:, lo + MLA_NOPE:lo + MLA_HEAD_W] = kr
        lo = h * MLA_V_W
        v_ref[:, lo:lo + MLA_V] = acc[:, v_first + h * MLA_V:v_first + (h + 1) * MLA_V].astype(BF16)
        v_ref[:, lo + MLA_V:lo + MLA_V_W] = ones


def _mla_kv_proj(ckv, w_bf, layer, kr):
    m = ckv.shape[0]
    bm = min(512, m)
    return pl.pallas_call(
        _mla_kv_kernel,
        grid=(m // bm,),
        in_specs=[
            pl.BlockSpec((bm, MLA_KV_RANK), lambda i: (i, 0)),
            pl.BlockSpec((None, MLA_KV_RANK, MLA_HEADS * (MLA_NOPE + MLA_V)), lambda i: (layer, 0, 0)),
            pl.BlockSpec((bm, 128), lambda i: (i, 0)),
        ],
        out_specs=[
            pl.BlockSpec((bm, MLA_HEADS * MLA_HEAD_W), lambda i: (i, 0)),
            pl.BlockSpec((bm, MLA_HEADS * MLA_V_W), lambda i: (i, 0)),
        ],
        out_shape=[
            jax.ShapeDtypeStruct((m, MLA_HEADS * MLA_HEAD_W), BF16),
            jax.ShapeDtypeStruct((m, MLA_HEADS * MLA_V_W), BF16),
        ],
        compiler_params=_params(("parallel",)),
        name="mla_kv_proj",
    )(ckv, w_bf, kr)


def _attn_kernel(q_ref, k_ref, v_ref, o_ref, *, kc, group):
    q = q_ref[0]
    bq = q.shape[0]
    n_groups = k_ref.shape[1] // (kc * group)

    def kv_rows(g, c):
        return pl.ds(pl.multiple_of((g * group + c) * kc, kc), kc)

    def body(g, carry):
        m, acc = carry
        s_next = _dot_nt(q, k_ref[0, kv_rows(g, 0), :])
        for c in range(group):
            s = s_next
            if c + 1 < group:
                s_next = _dot_nt(q, k_ref[0, kv_rows(g, c + 1), :])
            m_new = jnp.maximum(m, jnp.max(s, axis=-1, keepdims=True))
            alpha = jnp.exp2(m - m_new)
            p = jnp.exp2(s - m_new).astype(BF16)
            acc = alpha * acc + _dot(p, v_ref[0, kv_rows(g, c), :])
            m = m_new
        return m, acc

    m0 = jnp.full((bq, 1), -jnp.inf, F32)
    a0 = jnp.zeros((bq, MLA_V_W), F32)
    _, acc = lax.fori_loop(0, n_groups, body, (m0, a0))
    o_ref[0] = (acc[:, :MLA_V] / acc[:, MLA_V:]).astype(BF16)


def _attention(q, k, v, batch, seq):
    bq = min(1024, seq)
    kc = min(512, seq)
    group = min(16, seq // kc)
    q3 = q.reshape(batch, seq, MLA_HEADS * MLA_HEAD_W)
    k3 = k.reshape(batch, seq, MLA_HEADS * MLA_HEAD_W)
    v3 = v.reshape(batch, seq, MLA_HEADS * MLA_V_W)
    return pl.pallas_call(
        functools.partial(_attn_kernel, kc=kc, group=group),
        grid=(batch, MLA_HEADS, seq // bq),
        in_specs=[
            pl.BlockSpec((1, bq, MLA_HEAD_W), lambda b, h, i: (b, i, h)),
            pl.BlockSpec((1, seq, MLA_HEAD_W), lambda b, h, i: (b, 0, h)),
            pl.BlockSpec((1, seq, MLA_V_W), lambda b, h, i: (b, 0, h)),
        ],
        out_specs=pl.BlockSpec((1, bq, MLA_V), lambda b, h, i: (b, i, h)),
        out_shape=jax.ShapeDtypeStruct((batch, seq, MLA_HEADS * MLA_V), BF16),
        compiler_params=_params(("parallel", "parallel", "arbitrary")),
        name="attention",
    )(q3, k3, v3).reshape(batch * seq, MLA_HEADS * MLA_V)


def _ffn_halo_kernel(x_ref, w_ref, o_ref):
    o_ref[...] = _dot(x_ref[...], w_ref[...])


def _ffn_halo(x_rows, w_up_bf, layer):
    r = x_rows.shape[0]
    bn = 512
    g_first = FFN_DIM // bn
    return pl.pallas_call(
        _ffn_halo_kernel,
        grid=(FFN_DIM // bn,),
        in_specs=[
            pl.BlockSpec((r, D_MODEL), lambda j: (0, 0)),
            pl.BlockSpec((None, D_MODEL, bn), lambda j: (layer, 0, g_first + j)),
        ],
        out_specs=pl.BlockSpec((r, bn), lambda j: (0, j)),
        out_shape=jax.ShapeDtypeStruct((r, FFN_DIM), F32),
        compiler_params=_params(("parallel",)),
        name="ffn_halo",
    )(x_rows, w_up_bf)


def _ffn_up_kernel(x_ref, wu_ref, wg_ref, cw_ref, cb_ref, hp_ref, hn_ref, o_ref):
    x = x_ref[...]
    bm = x.shape[0]
    for c in range(o_ref.shape[1] // FFN_SLAB):
        cols = slice(c * FFN_SLAB, (c + 1) * FFN_SLAB)
        u = _dot(x, wu_ref[:, cols])
        g = _dot(x, wg_ref[:, cols])
        row = lax.broadcasted_iota(jnp.int32, g.shape, 0)
        g_prev = jnp.where(row == 0, hp_ref[0, :, cols], pltpu.roll(g, 1, axis=0))
        g_next = jnp.where(row == bm - 1, hn_ref[0, :, cols], pltpu.roll(g, bm - 1, axis=0))
        gc = (g_prev * cw_ref[0:1, cols] + g * cw_ref[1:2, cols] + g_next * cw_ref[2:3, cols]
              + cb_ref[:, cols])
        o_ref[:, cols] = (gc * jax.nn.sigmoid(gc) * u).astype(BF16)


def _ffn_up(x_bf, w_up_bf, layer, conv_w, conv_b, seq):
    m = x_bf.shape[0]
    bm = min(1024, seq)
    bn = 512
    nmb = m // bm
    g_first = FFN_DIM // bn
    xr = x_bf.reshape(nmb, bm, D_MODEL)
    tile_start = np.arange(nmb) * bm
    has_prev = jnp.asarray((tile_start % seq) != 0)[:, None]
    has_next = jnp.asarray(((tile_start + bm) % seq) != 0)[:, None]
    x_prev = jnp.where(has_prev, jnp.roll(xr[:, bm - 1, :], 1, axis=0), 0).astype(BF16)
    x_next = jnp.where(has_next, jnp.roll(xr[:, 0, :], -1, axis=0), 0).astype(BF16)
    pad = (-2 * nmb) % 16
    rows = jnp.concatenate([x_prev, x_next, jnp.zeros((pad, D_MODEL), BF16)], axis=0)
    halo = _ffn_halo(rows, w_up_bf, layer)
    h_prev = halo[:nmb].reshape(nmb, 1, FFN_DIM)
    h_next = halo[nmb:2 * nmb].reshape(nmb, 1, FFN_DIM)
    return pl.pallas_call(
        _ffn_up_kernel,
        grid=(nmb, FFN_DIM // bn),
        in_specs=[
            pl.BlockSpec((bm, D_MODEL), lambda i, j: (i, 0)),
            pl.BlockSpec((None, D_MODEL, bn), lambda i, j: (layer, 0, j)),
            pl.BlockSpec((None, D_MODEL, bn), lambda i, j: (layer, 0, g_first + j)),
            pl.BlockSpec((3, bn), lambda i, j: (0, j)),
            pl.BlockSpec((1, bn), lambda i, j: (0, j)),
            pl.BlockSpec((1, 1, bn), lambda i, j: (i, 0, j)),
            pl.BlockSpec((1, 1, bn), lambda i, j: (i, 0, j)),
        ],
        out_specs=pl.BlockSpec((bm, bn), lambda i, j: (i, j)),
        out_shape=jax.ShapeDtypeStruct((m, FFN_DIM), BF16),
        compiler_params=_params(("parallel", "arbitrary")),
        name="ffn_up",
    )(x_bf, w_up_bf, w_up_bf, conv_w, conv_b.reshape(1, FFN_DIM), h_prev, h_next)


def _rope_tables(seq, dim, base):
    inv = 1.0 / (base ** (jnp.arange(0, dim // 2, dtype=F32) * (2.0 / dim)))
    ang = jnp.arange(seq, dtype=F32)[:, None] * inv[None, :]
    return jnp.cos(ang), jnp.sin(ang)


def _rotary_quad(r):
    half = MLA_ROPE // 2
    x1, x2 = r[..., :half], r[..., half:]
    return jnp.concatenate([x1, x2, x2, x1], axis=-1)


def _mla_weight_layouts(w_in, w_uq, w_ukv):
    n_layers = w_in.shape[0]
    lat = MLA_Q_RANK + MLA_KV_RANK
    w_in = jnp.concatenate([w_in[..., :lat], _rotary_quad(w_in[..., lat:])], axis=-1)
    q = w_uq.reshape(n_layers, MLA_Q_RANK, MLA_HEADS, MLA_NOPE + MLA_ROPE)
    q = jnp.concatenate([q[..., :MLA_NOPE], _rotary_quad(q[..., MLA_NOPE:])], axis=-1)
    w_uq = q.reshape(n_layers, MLA_Q_RANK, MLA_HEADS * MLA_HEAD_W)
    kv = w_ukv.reshape(n_layers, MLA_KV_RANK, MLA_HEADS, MLA_NOPE + MLA_V)
    w_ukv = jnp.concatenate([kv[..., :MLA_NOPE].reshape(n_layers, MLA_KV_RANK, MLA_HEADS * MLA_NOPE),
                             kv[..., MLA_NOPE:].reshape(n_layers, MLA_KV_RANK, MLA_HEADS * MLA_V)],
                            axis=-1)
    return w_in, w_uq, w_ukv


def _trunk(x, seq, w):
    batch = x.shape[0]
    m = batch * seq
    x_f = x.reshape(m, D_MODEL)
    x_b = x_f.astype(BF16)

    ret_cos, ret_sin = _rope_tables(seq, RET_QK_DIM, RET_ROPE_BASE)
    cos, sin = _rope_tables(seq, MLA_ROPE, MLA_ROPE_BASE)
    rot = jnp.concatenate([cos, cos, -sin, sin], axis=1)
    score_scale = (MLA_NOPE + MLA_ROPE) ** -0.5 * LOG2_E
    qtab = jnp.concatenate([jnp.ones((seq, MLA_NOPE), F32), rot], axis=1) * score_scale

    for i in range(DEPTH):
        j = i // 2
        if i % 2 == 0:
            p = _ret_in_proj(x_b, w["ret_w_in"], j, ret_cos, ret_sin, seq)
            mix = _retention(p, w["ret_decay_fwd"], w["ret_decay_bwd"], j, batch, seq)
            x_f, x_b = _mm_res_ln(mix, w["ret_w_out"], j, x_f, w["ln1_g"][i], w["ln1_b"][i])
        else:
            cq, ckv, kr = _mla_in_proj(x_b, w["mla_w_in"], j, w["mla_q_norm"][j],
                                       w["mla_kv_norm"][j], rot, seq)
            q = _mla_q_proj(cq, w["mla_w_uq"], j, qtab, seq)
            k, v = _mla_kv_proj(ckv, w["mla_w_ukv"], j, kr)
            mix = _attention(q, k, v, batch, seq)
            x_f, x_b = _mm_res_ln(mix, w["mla_w_out"], j, x_f, w["ln1_g"][i], w["ln1_b"][i])
        a = _ffn_up(x_b, w["ffn_w_up"], i, w["ffn_conv_w"][i], w["ffn_conv_b"][i], seq)
        x_f, x_b = _mm_res_ln(a, w["ffn_w_down"], i, x_f, w["ln2_g"][i], w["ln2_b"][i])
    return x_f.reshape(batch, seq, D_MODEL)


def _prepare_weights(ret_w_in, ret_decay_fwd, ret_decay_bwd, ret_w_out,
                     mla_w_in, mla_q_norm, mla_kv_norm, mla_w_uq, mla_w_ukv, mla_w_out,
                     ln1_g, ln1_b, ln2_g, ln2_b, ffn_w_up, ffn_conv_w, ffn_conv_b, ffn_w_down):
    mla_w_in, mla_w_uq, mla_w_ukv = _mla_weight_layouts(mla_w_in, mla_w_uq, mla_w_ukv)

    def lanes(d):
        return jnp.broadcast_to(d.astype(F32)[:, :, None, None], d.shape + (1, 128))

    return {
        "ret_w_in": ret_w_in.astype(BF16),
        "ret_decay_fwd": lanes(ret_decay_fwd),
        "ret_decay_bwd": lanes(ret_decay_bwd),
        "ret_w_out": ret_w_out.astype(BF16),
        "mla_w_in": mla_w_in.astype(BF16),
        "mla_q_norm": mla_q_norm.astype(F32),
        "mla_kv_norm": mla_kv_norm.astype(F32),
        "mla_w_uq": mla_w_uq.astype(BF16),
        "mla_w_ukv": mla_w_ukv.astype(BF16),
        "mla_w_out": mla_w_out.astype(BF16),
        "ln1_g": ln1_g.astype(F32), "ln1_b": ln1_b.astype(F32),
        "ln2_g": ln2_g.astype(F32), "ln2_b": ln2_b.astype(F32),
        "ffn_w_up": ffn_w_up.astype(BF16),
        "ffn_conv_w": ffn_conv_w.astype(F32),
        "ffn_conv_b": ffn_conv_b.astype(F32),
        "ffn_w_down": ffn_w_down.astype(BF16),
    }


def kernel(x_prompt, x_sample, ret_w_in, ret_decay_fwd, ret_decay_bwd, ret_w_out, mla_w_in, mla_q_norm, mla_kv_norm, mla_w_uq, mla_w_ukv, mla_w_out, ln1_g, ln1_b, ln2_g, ln2_b, ffn_w_up, ffn_conv_w, ffn_conv_b, ffn_w_down):
    w = _prepare_weights(ret_w_in, ret_decay_fwd, ret_decay_bwd, ret_w_out,
                         mla_w_in, mla_q_norm, mla_kv_norm, mla_w_uq, mla_w_ukv, mla_w_out,
                         ln1_g, ln1_b, ln2_g, ln2_b, ffn_w_up, ffn_conv_w, ffn_conv_b, ffn_w_down)
    y_prompt = _trunk(x_prompt, x_prompt.shape[1], w)
    y_sample = _trunk(x_sample, x_sample.shape[1], w)
    return (y_prompt, y_sample)
```

```python
import functools
import math

import jax
import jax.numpy as jnp
import numpy as np
from jax import lax
from jax.experimental import pallas as pl
from jax.experimental.pallas import tpu as pltpu

F32 = jnp.float32
BF16 = jnp.bfloat16

D_MODEL = 2048
DEPTH = 4
ALPHA = (2.0 * DEPTH) ** 0.25

RET_HEADS = 8
RET_QK_DIM = 256
RET_V_DIM = 512
RET_QK_W = RET_HEADS * RET_QK_DIM
RET_V_W = RET_HEADS * RET_V_DIM
RET_IN = 2 * RET_QK_W + 2 * RET_V_W
RET_ROPE_BASE = 10000.0

MLA_HEADS = 16
MLA_Q_RANK = 1536
MLA_KV_RANK = 512
MLA_NOPE = 128
MLA_ROPE = 64
MLA_V = 128
MLA_ROPE_BASE = 10000.0
MLA_HEAD_W = 256
MLA_IN_W = MLA_Q_RANK + MLA_KV_RANK + 128
MLA_V_W = 256

FFN_DIM = 5632
FFN_GROUP = 128

LN_EPS = 1e-5
RMS_EPS = 1e-6
GN_EPS = 1e-6

VMEM_LIMIT_BYTES = 56 * 1024 * 1024
LOG2_E = 1.4426950408889634


def _params(semantics):
    return pltpu.CompilerParams(dimension_semantics=semantics,
                                vmem_limit_bytes=VMEM_LIMIT_BYTES)


def _dot(a, b):
    return jnp.dot(a, b, preferred_element_type=F32)


def _dot_nt(a, b):
    return lax.dot_general(a, b, (((1,), (1,)), ((), ())), preferred_element_type=F32)


def _dot_tn(a, b):
    return lax.dot_general(a, b, (((0,), (0,)), ((), ())), preferred_element_type=F32)


def _ret_in_kernel(x_ref, w_ref, cos_ref, sin_ref, o_ref, *, bn):
    j = pl.program_id(1)
    n_qk_tiles = 2 * RET_QK_W // bn

    @pl.when(j < n_qk_tiles)
    def _():
        cos = cos_ref[...]
        sin = sin_ref[...]
        scale = jnp.where(j >= n_qk_tiles // 2, RET_QK_DIM ** -0.5, 1.0).astype(F32)
        half = RET_QK_DIM // 2
        for h in range(bn // RET_QK_DIM):
            lo = h * RET_QK_DIM
            acc = _dot(x_ref[...], w_ref[:, lo:lo + RET_QK_DIM])
            x1 = acc[:, :half]
            x2 = acc[:, half:]
            o_ref[:, lo:lo + half] = ((x1 * cos - x2 * sin) * scale).astype(BF16)
            o_ref[:, lo + half:lo + 2 * half] = ((x1 * sin + x2 * cos) * scale).astype(BF16)

    n_v_tiles = RET_V_W // bn

    @pl.when((j >= n_qk_tiles) & (j < n_qk_tiles + n_v_tiles))
    def _():
        for h in range(bn // RET_QK_DIM):
            cols = slice(h * RET_QK_DIM, (h + 1) * RET_QK_DIM)
            o_ref[:, cols] = _dot(x_ref[...], w_ref[:, cols]).astype(BF16)

    @pl.when(j >= n_qk_tiles + n_v_tiles)
    def _():
        for h in range(bn // RET_QK_DIM):
            cols = slice(h * RET_QK_DIM, (h + 1) * RET_QK_DIM)
            gate = _dot(x_ref[...], w_ref[:, cols])
            o_ref[:, cols] = (gate * jax.nn.sigmoid(gate)).astype(BF16)


def _ret_in_proj(x_bf, w_bf, layer, cos, sin, seq):
    m = x_bf.shape[0]
    bm = min(2048, seq)
    bn = 1024
    n_pos_blocks = seq // bm
    return pl.pallas_call(
        functools.partial(_ret_in_kernel, bn=bn),
        grid=(m // bm, RET_IN // bn),
        in_specs=[
            pl.BlockSpec((bm, D_MODEL), lambda i, j: (i, 0)),
            pl.BlockSpec((None, D_MODEL, bn), lambda i, j: (layer, 0, j)),
            pl.BlockSpec((bm, RET_QK_DIM // 2), lambda i, j: (i % n_pos_blocks, 0)),
            pl.BlockSpec((bm, RET_QK_DIM // 2), lambda i, j: (i % n_pos_blocks, 0)),
        ],
        out_specs=pl.BlockSpec((bm, bn), lambda i, j: (i, j)),
        out_shape=jax.ShapeDtypeStruct((m, RET_IN), BF16),
        compiler_params=_params(("parallel", "arbitrary")),
        name="ret_in_proj",
    )(x_bf, w_bf, cos, sin)


def _log_sigmoid(x):
    return jnp.minimum(x, 0.0) - jnp.log1p(jnp.exp(-jnp.abs(x)))


def _retention_kernel(q_ref, k_ref, v_ref, g_ref, df_ref, db_ref, o_ref,
                      state_ref, obuf_ref, dmat_ref, xi_ref, zeta_ref, *, chunk, n_sub):
    ps = pl.program_id(2)
    t = pl.program_id(3)
    nt = pl.num_programs(3)
    c = chunk

    @pl.when(t == 0)
    def _():
        state_ref[...] = jnp.zeros_like(state_ref)

    @pl.when((t == 0) & (ps == 0))
    def _():
        lf = _log_sigmoid(df_ref[...])
        lb = _log_sigmoid(db_ref[...])
        row = lax.broadcasted_iota(jnp.int32, (c, 128), 0).astype(F32)
        xi_ref[0] = jnp.exp(lb * (c - row))
        xi_ref[1] = jnp.exp(lf * (row + 1.0))
        zeta_ref[0] = jnp.exp(lb * row)
        zeta_ref[1] = jnp.exp(lf * (c - 1.0 - row))
        for jb in range(c // 128):
            col = lax.broadcasted_iota(jnp.int32, (c, 128), 1).astype(F32) + (128.0 * jb)
            rel = row - col
            dmat_ref[:, jb * 128:(jb + 1) * 128] = jnp.where(
                rel >= 0.0, jnp.exp(lf * jnp.maximum(rel, 0.0)), jnp.exp(lb * jnp.maximum(-rel, 0.0)))

    def tile128(tab, width):
        return jnp.concatenate([tab] * (width // 128), axis=1)

    def state_step(qc, kc, vc, direction):
        xi = xi_ref[direction]
        zeta = zeta_ref[direction]
        st = state_ref[...]
        inter = _dot(qc, st.astype(BF16)) * tile128(xi, RET_V_DIM)
        kz = (kc.astype(F32) * tile128(zeta, RET_QK_DIM)).astype(BF16)
        g_chunk = xi[c - 1:c, :] if direction == 1 else xi[0:1, :]
        state_ref[...] = st * tile128(g_chunk, RET_V_DIM) + _dot_tn(kz, vc)
        return inter

    @pl.when(ps == 0)
    def _():
        for s in reversed(range(n_sub)):
            rows = pl.ds(s * c, c)
            inter = state_step(q_ref[0, rows, :], k_ref[0, rows, :], v_ref[0, rows, :], 0)
            base = pl.multiple_of(((nt - 1 - t) * n_sub + s) * c, c)
            obuf_ref[pl.ds(base, c), :] = inter

    @pl.when(ps == 1)
    def _():
        for s in range(n_sub):
            rows = pl.ds(s * c, c)
            qc = q_ref[0, rows, :]
            kc = k_ref[0, rows, :]
            vc = v_ref[0, rows, :]
            sc = (_dot_nt(qc, kc) * dmat_ref[...]).astype(BF16)
            base = pl.multiple_of((t * n_sub + s) * c, c)
            o = _dot(sc, vc) + obuf_ref[pl.ds(base, c), :]
            o = o + state_step(qc, kc, vc, 1)
            mu = jnp.mean(o, axis=-1, keepdims=True)
            d = o - mu
            var = jnp.mean(d * d, axis=-1, keepdims=True)
            o = d * lax.rsqrt(var + GN_EPS)
            o_ref[0, rows, :] = (g_ref[0, rows, :].astype(F32) * o).astype(BF16)


def _retention(p, decay_f, decay_b, layer, batch, seq):
    chunk = min(256, seq)
    tb = min(2048, seq)
    n_sub = tb // chunk
    nt = seq // tb
    p3 = p.reshape(batch, seq, RET_IN)
    qk_blocks = RET_QK_W // RET_QK_DIM
    v_first = 2 * RET_QK_W // RET_V_DIM
    g_first = v_first + RET_HEADS

    def pos(ps, t):
        return jnp.where(ps == 0, nt - 1 - t, t)

    return pl.pallas_call(
        functools.partial(_retention_kernel, chunk=chunk, n_sub=n_sub),
        grid=(batch, RET_HEADS, 2, nt),
        in_specs=[
            pl.BlockSpec((1, tb, RET_QK_DIM), lambda b, h, ps, t: (b, pos(ps, t), h)),
            pl.BlockSpec((1, tb, RET_QK_DIM), lambda b, h, ps, t: (b, pos(ps, t), qk_blocks + h)),
            pl.BlockSpec((1, tb, RET_V_DIM), lambda b, h, ps, t: (b, pos(ps, t), v_first + h)),
            pl.BlockSpec((1, tb, RET_V_DIM), lambda b, h, ps, t: (b, t * ps, g_first + h)),
            pl.BlockSpec((None, None, 1, 128), lambda b, h, ps, t: (layer, h, 0, 0)),
            pl.BlockSpec((None, None, 1, 128), lambda b, h, ps, t: (layer, h, 0, 0)),
        ],
        out_specs=pl.BlockSpec((1, tb, RET_V_DIM), lambda b, h, ps, t: (b, t * ps, h)),
        out_shape=jax.ShapeDtypeStruct((batch, seq, RET_V_W), BF16),
        scratch_shapes=[
            pltpu.VMEM((RET_QK_DIM, RET_V_DIM), F32),
            pltpu.VMEM((seq, RET_V_DIM), F32),
            pltpu.VMEM((chunk, chunk), F32),
            pltpu.VMEM((2, chunk, 128), F32),
            pltpu.VMEM((2, chunk, 128), F32),
        ],
        compiler_params=_params(("parallel", "parallel", "arbitrary", "arbitrary")),
        name="retention",
    )(p3, p3, p3, p3, decay_f, decay_b).reshape(batch * seq, RET_V_W)


def _residual_layer_norm(x_ref, g_ref, b_ref, of_ref, ob_ref, ln_rows):
    for r in range(of_ref.shape[0] // ln_rows):
        rows = slice(r * ln_rows, (r + 1) * ln_rows)
        y = ALPHA * x_ref[rows, :] + of_ref[rows, :]
        mu = jnp.mean(y, axis=-1, keepdims=True)
        d = y - mu
        var = jnp.mean(d * d, axis=-1, keepdims=True)
        y = d * lax.rsqrt(var + LN_EPS) * g_ref[...] + b_ref[...]
        of_ref[rows, :] = y
        ob_ref[rows, :] = y.astype(BF16)


def _mm_res_ln_kernel(a_ref, w_ref, x_ref, g_ref, b_ref, of_ref, ob_ref, *, ln_rows, n_k):
    if n_k == 1:
        of_ref[...] = _dot(a_ref[...], w_ref[...])
        _residual_layer_norm(x_ref, g_ref, b_ref, of_ref, ob_ref, ln_rows)
        return

    kk = pl.program_id(1)

    @pl.when(kk == 0)
    def _():
        of_ref[...] = _dot(a_ref[...], w_ref[...])

    @pl.when((kk > 0) & (kk < n_k - 1))
    def _():
        of_ref[...] += _dot(a_ref[...], w_ref[...])

    @pl.when(kk == n_k - 1)
    def _():
        of_ref[...] += _dot(a_ref[...], w_ref[...])
        _residual_layer_norm(x_ref, g_ref, b_ref, of_ref, ob_ref, ln_rows)


RESIDENT_WEIGHT_BYTES = 16 * 1024 * 1024


def _mm_res_ln(a_bf, w_bf, layer, x_f32, g, b):
    m, k = a_bf.shape
    if k * D_MODEL * 2 <= RESIDENT_WEIGHT_BYTES:
        bm, bk = min(512, m), k
        w_spec = pl.BlockSpec((None, bk, D_MODEL), lambda i, kk: (layer, 0, 0),
                              pipeline_mode=pl.Buffered(1))
    else:
        bm, bk = min(1024, m), 512
        w_spec = pl.BlockSpec((None, bk, D_MODEL), lambda i, kk: (layer, kk, 0))
    n_k = k // bk
    return pl.pallas_call(
        functools.partial(_mm_res_ln_kernel, ln_rows=min(256, bm), n_k=n_k),
        grid=(m // bm, n_k),
        in_specs=[
            pl.BlockSpec((bm, bk), lambda i, kk: (i, kk)),
            w_spec,
            pl.BlockSpec((bm, D_MODEL), lambda i, kk: (i, 0)),
            pl.BlockSpec((1, D_MODEL), lambda i, kk: (0, 0)),
            pl.BlockSpec((1, D_MODEL), lambda i, kk: (0, 0)),
        ],
        out_specs=[
            pl.BlockSpec((bm, D_MODEL), lambda i, kk: (i, 0)),
            pl.BlockSpec((bm, D_MODEL), lambda i, kk: (i, 0)),
        ],
        out_shape=[
            jax.ShapeDtypeStruct((m, D_MODEL), F32),
            jax.ShapeDtypeStruct((m, D_MODEL), BF16),
        ],
        compiler_params=_params(("parallel", "arbitrary")),
        name="mm_res_ln",
    )(a_bf, w_bf, x_f32, g.reshape(1, D_MODEL), b.reshape(1, D_MODEL))


def _rms(x, g):
    return x * lax.rsqrt(jnp.mean(x * x, axis=-1, keepdims=True) + RMS_EPS) * g


def _mla_in_kernel(x_ref, w_ref, qg_ref, kg_ref, tab_ref, cq_ref, ckv_ref, kr_ref):
    acc = _dot(x_ref[...], w_ref[...])
    cq_ref[...] = _rms(acc[:, :MLA_Q_RANK], qg_ref[...]).astype(BF16)
    ckv_ref[...] = _rms(acc[:, MLA_Q_RANK:MLA_Q_RANK + MLA_KV_RANK], kg_ref[...]).astype(BF16)
    part = acc[:, MLA_Q_RANK + MLA_KV_RANK:] * tab_ref[...]
    kr_ref[...] = (part + pltpu.roll(part, 64, axis=1)).astype(BF16)


def _mla_in_proj(x_bf, w_bf, layer, q_norm, kv_norm, ktab, seq):
    m = x_bf.shape[0]
    bm = min(512, seq)
    n_pos_blocks = seq // bm
    return pl.pallas_call(
        _mla_in_kernel,
        grid=(m // bm,),
        in_specs=[
            pl.BlockSpec((bm, D_MODEL), lambda i: (i, 0)),
            pl.BlockSpec((None, D_MODEL, MLA_IN_W), lambda i: (layer, 0, 0)),
            pl.BlockSpec((1, MLA_Q_RANK), lambda i: (0, 0)),
            pl.BlockSpec((1, MLA_KV_RANK), lambda i: (0, 0)),
            pl.BlockSpec((bm, 128), lambda i: (i % n_pos_blocks, 0)),
        ],
        out_specs=[
            pl.BlockSpec((bm, MLA_Q_RANK), lambda i: (i, 0)),
            pl.BlockSpec((bm, MLA_KV_RANK), lambda i: (i, 0)),
            pl.BlockSpec((bm, 128), lambda i: (i, 0)),
        ],
        out_shape=[
            jax.ShapeDtypeStruct((m, MLA_Q_RANK), BF16),
            jax.ShapeDtypeStruct((m, MLA_KV_RANK), BF16),
            jax.ShapeDtypeStruct((m, 128), BF16),
        ],
        compiler_params=_params(("parallel",)),
        name="mla_in_proj",
    )(x_bf, w_bf, q_norm.reshape(1, -1), kv_norm.reshape(1, -1), ktab)


def _mla_q_kernel(c_ref, w_ref, tab_ref, o_ref, *, heads_per_tile):
    acc = _dot(c_ref[...], w_ref[...])
    tab = tab_ref[...]
    for h in range(heads_per_tile):
        lo = h * MLA_HEAD_W
        o_ref[:, lo:lo + MLA_HEAD_W] = (acc[:, lo:lo + MLA_HEAD_W] * tab).astype(BF16)


def _mla_q_proj(cq, w_bf, layer, qtab, seq):
    m = cq.shape[0]
    bm = min(1024, seq)
    bn = 1024
    n_pos_blocks = seq // bm
    n_out = MLA_HEADS * MLA_HEAD_W
    return pl.pallas_call(
        functools.partial(_mla_q_kernel, heads_per_tile=bn // MLA_HEAD_W),
        grid=(m // bm, n_out // bn),
        in_specs=[
            pl.BlockSpec((bm, MLA_Q_RANK), lambda i, j: (i, 0)),
            pl.BlockSpec((None, MLA_Q_RANK, bn), lambda i, j: (layer, 0, j)),
            pl.BlockSpec((bm, MLA_HEAD_W), lambda i, j: (i % n_pos_blocks, 0)),
        ],
        out_specs=pl.BlockSpec((bm, bn), lambda i, j: (i, j)),
        out_shape=jax.ShapeDtypeStruct((m, n_out), BF16),
        compiler_params=_params(("parallel", "arbitrary")),
        name="mla_q_proj",
    )(cq, w_bf, qtab)


def _mla_kv_kernel(c_ref, w_ref, kr_ref, k_ref, v_ref):
    acc = _dot(c_ref[...], w_ref[...])
    kr = kr_ref[...]
    ones = jnp.ones((acc.shape[0], MLA_V_W - MLA_V), BF16)
    v_first = MLA_HEADS * MLA_NOPE
    for h in range(MLA_HEADS):
        lo = h * MLA_HEAD_W
        k_ref[:, lo:lo + MLA_NOPE] = acc[:, h * MLA_NOPE:(h + 1) * MLA_NOPE].astype(BF16)
        k_ref[:, lo + MLA_NOPE:lo + MLA_HEAD_W] = kr
        lo = h * MLA_V_W
        v_ref[:, lo:lo + MLA_V] = acc[:, v_first + h * MLA_V:v_first + (h + 1) * MLA_V].astype(BF16)
        v_ref[:, lo + MLA_V:lo + MLA_V_W] = ones


def _mla_kv_proj(ckv, w_bf, layer, kr):
    m = ckv.shape[0]
    bm = min(512, m)
    return pl.pallas_call(
        _mla_kv_kernel,
        grid=(m // bm,),
        in_specs=[
            pl.BlockSpec((bm, MLA_KV_RANK), lambda i: (i, 0)),
            pl.BlockSpec((None, MLA_KV_RANK, MLA_HEADS * (MLA_NOPE + MLA_V)), lambda i: (layer, 0, 0)),
            pl.BlockSpec((bm, 128), lambda i: (i, 0)),
        ],
        out_specs=[
            pl.BlockSpec((bm, MLA_HEADS * MLA_HEAD_W), lambda i: (i, 0)),
            pl.BlockSpec((bm, MLA_HEADS * MLA_V_W), lambda i: (i, 0)),
        ],
        out_shape=[
            jax.ShapeDtypeStruct((m, MLA_HEADS * MLA_HEAD_W), BF16),
            jax.ShapeDtypeStruct((m, MLA_HEADS * MLA_V_W), BF16),
        ],
        compiler_params=_params(("parallel",)),
        name="mla_kv_proj",
    )(ckv, w_bf, kr)


def _attn_kernel(q_ref, k_ref, v_ref, o_ref, *, kc, group):
    q = q_ref[0]
    bq = q.shape[0]
    n_groups = k_ref.shape[1] // (kc * group)

    def kv_rows(g, c):
        return pl.ds(pl.multiple_of((g * group + c) * kc, kc), kc)

    def body(g, carry):
        m, acc = carry
        s_next = _dot_nt(q, k_ref[0, kv_rows(g, 0), :])
        for c in range(group):
            s = s_next
            if c + 1 < group:
                s_next = _dot_nt(q, k_ref[0, kv_rows(g, c + 1), :])
            m_new = jnp.maximum(m, jnp.max(s, axis=-1, keepdims=True))
            alpha = jnp.exp2(m - m_new)
            p = jnp.exp2(s - m_new).astype(BF16)
            acc = alpha * acc + _dot(p, v_ref[0, kv_rows(g, c), :])
            m = m_new
        return m, acc

    m0 = jnp.full((bq, 1), -jnp.inf, F32)
    a0 = jnp.zeros((bq, MLA_V_W), F32)
    _, acc = lax.fori_loop(0, n_groups, body, (m0, a0))
    o_ref[0] = (acc[:, :MLA_V] / acc[:, MLA_V:]).astype(BF16)


def _attention(q, k, v, batch, seq):
    bq = min(1024, seq)
    kc = min(512, seq)
    group = min(16, seq // kc)
    q3 = q.reshape(batch, seq, MLA_HEADS * MLA_HEAD_W)
    k3 = k.reshape(batch, seq, MLA_HEADS * MLA_HEAD_W)
    v3 = v.reshape(batch, seq, MLA_HEADS * MLA_V_W)
    return pl.pallas_call(
        functools.partial(_attn_kernel, kc=kc, group=group),
        grid=(batch, MLA_HEADS, seq // bq),
        in_specs=[
            pl.BlockSpec((1, bq, MLA_HEAD_W), lambda b, h, i: (b, i, h)),
            pl.BlockSpec((1, seq, MLA_HEAD_W), lambda b, h, i: (b, 0, h)),
            pl.BlockSpec((1, seq, MLA_V_W), lambda b, h, i: (b, 0, h)),
        ],
        out_specs=pl.BlockSpec((1, bq, MLA_V), lambda b, h, i: (b, i, h)),
        out_shape=jax.ShapeDtypeStruct((batch, seq, MLA_HEADS * MLA_V), BF16),
        compiler_params=_params(("parallel", "parallel", "arbitrary")),
        name="attention",
    )(q3, k3, v3).reshape(batch * seq, MLA_HEADS * MLA_V)


def _ffn_halo_kernel(x_ref, w_ref, o_ref):
    o_ref[...] = _dot(x_ref[...], w_ref[...])[:, FFN_GROUP:]


def _ffn_halo(x_rows, w_up_bf, layer):
    r = x_rows.shape[0]
    return pl.pallas_call(
        _ffn_halo_kernel,
        grid=(FFN_DIM // FFN_GROUP,),
        in_specs=[
            pl.BlockSpec((r, D_MODEL), lambda j: (0, 0)),
            pl.BlockSpec((None, D_MODEL, 2 * FFN_GROUP), lambda j: (layer, 0, j)),
        ],
        out_specs=pl.BlockSpec((r, FFN_GROUP), lambda j: (0, j)),
        out_shape=jax.ShapeDtypeStruct((r, FFN_DIM), F32),
        compiler_params=_params(("parallel",)),
        name="ffn_halo",
    )(x_rows, w_up_bf)


def _ffn_up_kernel(x_ref, w_ref, cw_ref, cb_ref, hp_ref, hn_ref, o_ref):
    x = x_ref[...]
    bm = x.shape[0]
    for t in range(o_ref.shape[1] // FFN_GROUP):
        acc = _dot(x, w_ref[:, 2 * t * FFN_GROUP:2 * (t + 1) * FFN_GROUP])
        u = acc[:, :FFN_GROUP]
        g = acc[:, FFN_GROUP:]
        cols = slice(t * FFN_GROUP, (t + 1) * FFN_GROUP)
        row = lax.broadcasted_iota(jnp.int32, g.shape, 0)
        g_prev = jnp.where(row == 0, hp_ref[0, :, cols], pltpu.roll(g, 1, axis=0))
        g_next = jnp.where(row == bm - 1, hn_ref[0, :, cols], pltpu.roll(g, bm - 1, axis=0))
        gc = (g_prev * cw_ref[0:1, cols] + g * cw_ref[1:2, cols] + g_next * cw_ref[2:3, cols]
              + cb_ref[:, cols])
        o_ref[:, cols] = (gc * jax.nn.sigmoid(gc) * u).astype(BF16)


def _ffn_up(x_bf, w_up_bf, layer, conv_w, conv_b, seq):
    m = x_bf.shape[0]
    bm = min(1024, seq)
    bn = FFN_DIM // 4
    nmb = m // bm
    xr = x_bf.reshape(nmb, bm, D_MODEL)
    tile_start = np.arange(nmb) * bm
    has_prev = jnp.asarray((tile_start % seq) != 0)[:, None]
    has_next = jnp.asarray(((tile_start + bm) % seq) != 0)[:, None]
    x_prev = jnp.where(has_prev, jnp.roll(xr[:, bm - 1, :], 1, axis=0), 0).astype(BF16)
    x_next = jnp.where(has_next, jnp.roll(xr[:, 0, :], -1, axis=0), 0).astype(BF16)
    pad = (-2 * nmb) % 16
    rows = jnp.concatenate([x_prev, x_next, jnp.zeros((pad, D_MODEL), BF16)], axis=0)
    halo = _ffn_halo(rows, w_up_bf, layer)
    h_prev = halo[:nmb].reshape(nmb, 1, FFN_DIM)
    h_next = halo[nmb:2 * nmb].reshape(nmb, 1, FFN_DIM)
    return pl.pallas_call(
        _ffn_up_kernel,
        grid=(nmb, FFN_DIM // bn),
        in_specs=[
            pl.BlockSpec((bm, D_MODEL), lambda i, j: (i, 0)),
            pl.BlockSpec((None, D_MODEL, 2 * bn), lambda i, j: (layer, 0, j)),
            pl.BlockSpec((3, bn), lambda i, j: (0, j)),
            pl.BlockSpec((1, bn), lambda i, j: (0, j)),
            pl.BlockSpec((1, 1, bn), lambda i, j: (i, 0, j)),
            pl.BlockSpec((1, 1, bn), lambda i, j: (i, 0, j)),
        ],
        out_specs=pl.BlockSpec((bm, bn), lambda i, j: (i, j)),
        out_shape=jax.ShapeDtypeStruct((m, FFN_DIM), BF16),
        compiler_params=_params(("parallel", "arbitrary")),
        name="ffn_up",
    )(x_bf, w_up_bf, conv_w, conv_b.reshape(1, FFN_DIM), h_prev, h_next)


def _rope_tables(seq, dim, base):
    inv = 1.0 / (base ** (jnp.arange(0, dim // 2, dtype=F32) * (2.0 / dim)))
    ang = jnp.arange(seq, dtype=F32)[:, None] * inv[None, :]
    return jnp.cos(ang), jnp.sin(ang)


def _rotary_quad(r):
    half = MLA_ROPE // 2
    x1, x2 = r[..., :half], r[..., half:]
    return jnp.concatenate([x1, x2, x2, x1], axis=-1)


def _mla_weight_layouts(w_in, w_uq, w_ukv):
    n_layers = w_in.shape[0]
    lat = MLA_Q_RANK + MLA_KV_RANK
    w_in = jnp.concatenate([w_in[..., :lat], _rotary_quad(w_in[..., lat:])], axis=-1)
    q = w_uq.reshape(n_layers, MLA_Q_RANK, MLA_HEADS, MLA_NOPE + MLA_ROPE)
    q = jnp.concatenate([q[..., :MLA_NOPE], _rotary_quad(q[..., MLA_NOPE:])], axis=-1)
    w_uq = q.reshape(n_layers, MLA_Q_RANK, MLA_HEADS * MLA_HEAD_W)
    kv = w_ukv.reshape(n_layers, MLA_KV_RANK, MLA_HEADS, MLA_NOPE + MLA_V)
    w_ukv = jnp.concatenate([kv[..., :MLA_NOPE].reshape(n_layers, MLA_KV_RANK, MLA_HEADS * MLA_NOPE),
                             kv[..., MLA_NOPE:].reshape(n_layers, MLA_KV_RANK, MLA_HEADS * MLA_V)],
                            axis=-1)
    return w_in, w_uq, w_ukv


def _trunk(x, seq, w):
    batch = x.shape[0]
    m = batch * seq
    x_f = x.reshape(m, D_MODEL)
    x_b = x_f.astype(BF16)

    ret_cos, ret_sin = _rope_tables(seq, RET_QK_DIM, RET_ROPE_BASE)
    cos, sin = _rope_tables(seq, MLA_ROPE, MLA_ROPE_BASE)
    rot = jnp.concatenate([cos, cos, -sin, sin], axis=1)
    score_scale = (MLA_NOPE + MLA_ROPE) ** -0.5 * LOG2_E
    qtab = jnp.concatenate([jnp.ones((seq, MLA_NOPE), F32), rot], axis=1) * score_scale

    for i in range(DEPTH):
        j = i // 2
        if i % 2 == 0:
            p = _ret_in_proj(x_b, w["ret_w_in"], j, ret_cos, ret_sin, seq)
            mix = _retention(p, w["ret_decay_fwd"], w["ret_decay_bwd"], j, batch, seq)
            x_f, x_b = _mm_res_ln(mix, w["ret_w_out"], j, x_f, w["ln1_g"][i], w["ln1_b"][i])
        else:
            cq, ckv, kr = _mla_in_proj(x_b, w["mla_w_in"], j, w["mla_q_norm"][j],
                                       w["mla_kv_norm"][j], rot, seq)
            q = _mla_q_proj(cq, w["mla_w_uq"], j, qtab, seq)
            k, v = _mla_kv_proj(ckv, w["mla_w_ukv"], j, kr)
            mix = _attention(q, k, v, batch, seq)
            x_f, x_b = _mm_res_ln(mix, w["mla_w_out"], j, x_f, w["ln1_g"][i], w["ln1_b"][i])
        a = _ffn_up(x_b, w["ffn_w_up"], i, w["ffn_conv_w"][i], w["ffn_conv_b"][i], seq)
        x_f, x_b = _mm_res_ln(a, w["ffn_w_down"], i, x_f, w["ln2_g"][i], w["ln2_b"][i])
    return x_f.reshape(batch, seq, D_MODEL)


def _interleave_up_gate(w_up):
    n_layers, d, _ = w_up.shape
    halves = w_up.reshape(n_layers, d, 2, FFN_DIM // FFN_GROUP, FFN_GROUP)
    return jnp.swapaxes(halves, 2, 3).reshape(n_layers, d, 2 * FFN_DIM)


def _prepare_weights(ret_w_in, ret_decay_fwd, ret_decay_bwd, ret_w_out,
                     mla_w_in, mla_q_norm, mla_kv_norm, mla_w_uq, mla_w_ukv, mla_w_out,
                     ln1_g, ln1_b, ln2_g, ln2_b, ffn_w_up, ffn_conv_w, ffn_conv_b, ffn_w_down):
    mla_w_in, mla_w_uq, mla_w_ukv = _mla_weight_layouts(mla_w_in, mla_w_uq, mla_w_ukv)

    def lanes(d):
        return jnp.broadcast_to(d.astype(F32)[:, :, None, None], d.shape + (1, 128))

    return {
        "ret_w_in": ret_w_in.astype(BF16),
        "ret_decay_fwd": lanes(ret_decay_fwd),
        "ret_decay_bwd": lanes(ret_decay_bwd),
        "ret_w_out": ret_w_out.astype(BF16),
        "mla_w_in": mla_w_in.astype(BF16),
        "mla_q_norm": mla_q_norm.astype(F32),
        "mla_kv_norm": mla_kv_norm.astype(F32),
        "mla_w_uq": mla_w_uq.astype(BF16),
        "mla_w_ukv": mla_w_ukv.astype(BF16),
        "mla_w_out": mla_w_out.astype(BF16),
        "ln1_g": ln1_g.astype(F32), "ln1_b": ln1_b.astype(F32),
        "ln2_g": ln2_g.astype(F32), "ln2_b": ln2_b.astype(F32),
        "ffn_w_up": _interleave_up_gate(ffn_w_up).astype(BF16),
        "ffn_conv_w": ffn_conv_w.astype(F32),
        "ffn_conv_b": ffn_conv_b.astype(F32),
        "ffn_w_down": ffn_w_down.astype(BF16),
    }


def kernel(x_prompt, x_sample, ret_w_in, ret_decay_fwd, ret_decay_bwd, ret_w_out, mla_w_in, mla_q_norm, mla_kv_norm, mla_w_uq, mla_w_ukv, mla_w_out, ln1_g, ln1_b, ln2_g, ln2_b, ffn_w_up, ffn_conv_w, ffn_conv_b, ffn_w_down):
    w = _prepare_weights(ret_w_in, ret_decay_fwd, ret_decay_bwd, ret_w_out,
                         mla_w_in, mla_q_norm, mla_kv_norm, mla_w_uq, mla_w_ukv, mla_w_out,
                         ln1_g, ln1_b, ln2_g, ln2_b, ffn_w_up, ffn_conv_w, ffn_conv_b, ffn_w_down)
    y_prompt = _trunk(x_prompt, x_prompt.shape[1], w)
    y_sample = _trunk(x_sample, x_sample.shape[1], w)
    return (y_prompt, y_sample)
```

```python
import functools
import math

import jax
import jax.numpy as jnp
import numpy as np
from jax import lax
from jax.experimental import pallas as pl
from jax.experimental.pallas import tpu as pltpu

F32 = jnp.float32
BF16 = jnp.bfloat16

D_MODEL = 2048
DEPTH = 4
ALPHA = (2.0 * DEPTH) ** 0.25

RET_HEADS = 8
RET_QK_DIM = 256
RET_V_DIM = 512
RET_QK_W = RET_HEADS * RET_QK_DIM
RET_V_W = RET_HEADS * RET_V_DIM
RET_IN = 2 * RET_QK_W + 2 * RET_V_W
RET_ROPE_BASE = 10000.0

MLA_HEADS = 16
MLA_Q_RANK = 1536
MLA_KV_RANK = 512
MLA_NOPE = 128
MLA_ROPE = 64
MLA_V = 128
MLA_ROPE_BASE = 10000.0
MLA_HEAD_W = 256
MLA_IN_W = MLA_Q_RANK + MLA_KV_RANK + 128
MLA_V_W = 256

FFN_DIM = 5632
FFN_GROUP = 128

LN_EPS = 1e-5
RMS_EPS = 1e-6
GN_EPS = 1e-6

VMEM_LIMIT_BYTES = 56 * 1024 * 1024
LOG2_E = 1.4426950408889634


def _params(semantics):
    return pltpu.CompilerParams(dimension_semantics=semantics,
                                vmem_limit_bytes=VMEM_LIMIT_BYTES)


def _dot(a, b):
    return jnp.dot(a, b, preferred_element_type=F32)


def _dot_nt(a, b):
    return lax.dot_general(a, b, (((1,), (1,)), ((), ())), preferred_element_type=F32)


def _dot_tn(a, b):
    return lax.dot_general(a, b, (((0,), (0,)), ((), ())), preferred_element_type=F32)


def _ret_in_kernel(x_ref, w_ref, cos_ref, sin_ref, o_ref, *, bn):
    j = pl.program_id(1)
    n_qk_tiles = 2 * RET_QK_W // bn

    @pl.when(j < n_qk_tiles)
    def _():
        cos = cos_ref[...]
        sin = sin_ref[...]
        scale = jnp.where(j >= n_qk_tiles // 2, RET_QK_DIM ** -0.5, 1.0).astype(F32)
        half = RET_QK_DIM // 2
        for h in range(bn // RET_QK_DIM):
            lo = h * RET_QK_DIM
            acc = _dot(x_ref[...], w_ref[:, lo:lo + RET_QK_DIM])
            x1 = acc[:, :half]
            x2 = acc[:, half:]
            o_ref[:, lo:lo + half] = ((x1 * cos - x2 * sin) * scale).astype(BF16)
            o_ref[:, lo + half:lo + 2 * half] = ((x1 * sin + x2 * cos) * scale).astype(BF16)

    n_v_tiles = RET_V_W // bn

    @pl.when((j >= n_qk_tiles) & (j < n_qk_tiles + n_v_tiles))
    def _():
        for h in range(bn // RET_QK_DIM):
            cols = slice(h * RET_QK_DIM, (h + 1) * RET_QK_DIM)
            o_ref[:, cols] = _dot(x_ref[...], w_ref[:, cols]).astype(BF16)

    @pl.when(j >= n_qk_tiles + n_v_tiles)
    def _():
        for h in range(bn // RET_QK_DIM):
            cols = slice(h * RET_QK_DIM, (h + 1) * RET_QK_DIM)
            gate = _dot(x_ref[...], w_ref[:, cols])
            o_ref[:, cols] = (gate * jax.nn.sigmoid(gate)).astype(BF16)


def _ret_in_proj(x_bf, w_bf, layer, cos, sin, seq):
    m = x_bf.shape[0]
    bm = min(2048, seq)
    bn = 1024
    n_pos_blocks = seq // bm
    return pl.pallas_call(
        functools.partial(_ret_in_kernel, bn=bn),
        grid=(m // bm, RET_IN // bn),
        in_specs=[
            pl.BlockSpec((bm, D_MODEL), lambda i, j: (i, 0)),
            pl.BlockSpec((None, D_MODEL, bn), lambda i, j: (layer, 0, j)),
            pl.BlockSpec((bm, RET_QK_DIM // 2), lambda i, j: (i % n_pos_blocks, 0)),
            pl.BlockSpec((bm, RET_QK_DIM // 2), lambda i, j: (i % n_pos_blocks, 0)),
        ],
        out_specs=pl.BlockSpec((bm, bn), lambda i, j: (i, j)),
        out_shape=jax.ShapeDtypeStruct((m, RET_IN), BF16),
        compiler_params=_params(("parallel", "arbitrary")),
        name="ret_in_proj",
    )(x_bf, w_bf, cos, sin)


def _log_sigmoid(x):
    return jnp.minimum(x, 0.0) - jnp.log1p(jnp.exp(-jnp.abs(x)))


def _retention_kernel(q_ref, k_ref, v_ref, g_ref, df_ref, db_ref, o_ref,
                      state_ref, obuf_ref, dmat_ref, xi_ref, zeta_ref, *, chunk, n_sub):
    ps = pl.program_id(2)
    t = pl.program_id(3)
    nt = pl.num_programs(3)
    c = chunk

    @pl.when(t == 0)
    def _():
        state_ref[...] = jnp.zeros_like(state_ref)

    @pl.when((t == 0) & (ps == 0))
    def _():
        lf = _log_sigmoid(df_ref[...])
        lb = _log_sigmoid(db_ref[...])
        row = lax.broadcasted_iota(jnp.int32, (c, 128), 0).astype(F32)
        xi_ref[0] = jnp.exp(lb * (c - row))
        xi_ref[1] = jnp.exp(lf * (row + 1.0))
        zeta_ref[0] = jnp.exp(lb * row)
        zeta_ref[1] = jnp.exp(lf * (c - 1.0 - row))
        for jb in range(c // 128):
            col = lax.broadcasted_iota(jnp.int32, (c, 128), 1).astype(F32) + (128.0 * jb)
            rel = row - col
            dmat_ref[:, jb * 128:(jb + 1) * 128] = jnp.where(
                rel >= 0.0, jnp.exp(lf * jnp.maximum(rel, 0.0)), jnp.exp(lb * jnp.maximum(-rel, 0.0)))

    def tile128(tab, width):
        return jnp.concatenate([tab] * (width // 128), axis=1)

    def state_step(qc, kc, vc, direction):
        xi = xi_ref[direction]
        zeta = zeta_ref[direction]
        st = state_ref[...]
        inter = _dot(qc, st.astype(BF16)) * tile128(xi, RET_V_DIM)
        kz = (kc.astype(F32) * tile128(zeta, RET_QK_DIM)).astype(BF16)
        g_chunk = xi[c - 1:c, :] if direction == 1 else xi[0:1, :]
        state_ref[...] = st * tile128(g_chunk, RET_V_DIM) + _dot_tn(kz, vc)
        return inter

    @pl.when(ps == 0)
    def _():
        for s in reversed(range(n_sub)):
            rows = pl.ds(s * c, c)
            inter = state_step(q_ref[0, rows, :], k_ref[0, rows, :], v_ref[0, rows, :], 0)
            base = pl.multiple_of(((nt - 1 - t) * n_sub + s) * c, c)
            obuf_ref[pl.ds(base, c), :] = inter

    @pl.when(ps == 1)
    def _():
        for s in range(n_sub):
            rows = pl.ds(s * c, c)
            qc = q_ref[0, rows, :]
            kc = k_ref[0, rows, :]
            vc = v_ref[0, rows, :]
            sc = (_dot_nt(qc, kc) * dmat_ref[...]).astype(BF16)
            base = pl.multiple_of((t * n_sub + s) * c, c)
            o = _dot(sc, vc) + obuf_ref[pl.ds(base, c), :]
            o = o + state_step(qc, kc, vc, 1)
            mu = jnp.mean(o, axis=-1, keepdims=True)
            d = o - mu
            var = jnp.mean(d * d, axis=-1, keepdims=True)
            o = d * lax.rsqrt(var + GN_EPS)
            o_ref[0, rows, :] = (g_ref[0, rows, :].astype(F32) * o).astype(BF16)


def _retention(p, decay_f, decay_b, layer, batch, seq):
    chunk = min(256, seq)
    tb = min(2048, seq)
    n_sub = tb // chunk
    nt = seq // tb
    p3 = p.reshape(batch, seq, RET_IN)
    qk_blocks = RET_QK_W // RET_QK_DIM
    v_first = 2 * RET_QK_W // RET_V_DIM
    g_first = v_first + RET_HEADS

    def pos(ps, t):
        return jnp.where(ps == 0, nt - 1 - t, t)

    return pl.pallas_call(
        functools.partial(_retention_kernel, chunk=chunk, n_sub=n_sub),
        grid=(batch, RET_HEADS, 2, nt),
        in_specs=[
            pl.BlockSpec((1, tb, RET_QK_DIM), lambda b, h, ps, t: (b, pos(ps, t), h)),
            pl.BlockSpec((1, tb, RET_QK_DIM), lambda b, h, ps, t: (b, pos(ps, t), qk_blocks + h)),
            pl.BlockSpec((1, tb, RET_V_DIM), lambda b, h, ps, t: (b, pos(ps, t), v_first + h)),
            pl.BlockSpec((1, tb, RET_V_DIM), lambda b, h, ps, t: (b, t * ps, g_first + h)),
            pl.BlockSpec((None, None, 1, 128), lambda b, h, ps, t: (layer, h, 0, 0)),
            pl.BlockSpec((None, None, 1, 128), lambda b, h, ps, t: (layer, h, 0, 0)),
        ],
        out_specs=pl.BlockSpec((1, tb, RET_V_DIM), lambda b, h, ps, t: (b, t * ps, h)),
        out_shape=jax.ShapeDtypeStruct((batch, seq, RET_V_W), BF16),
        scratch_shapes=[
            pltpu.VMEM((RET_QK_DIM, RET_V_DIM), F32),
            pltpu.VMEM((seq, RET_V_DIM), F32),
            pltpu.VMEM((chunk, chunk), F32),
            pltpu.VMEM((2, chunk, 128), F32),
            pltpu.VMEM((2, chunk, 128), F32),
        ],
        compiler_params=_params(("parallel", "parallel", "arbitrary", "arbitrary")),
        name="retention",
    )(p3, p3, p3, p3, decay_f, decay_b).reshape(batch * seq, RET_V_W)


def _residual_layer_norm(x_ref, g_ref, b_ref, of_ref, ob_ref, ln_rows):
    for r in range(of_ref.shape[0] // ln_rows):
        rows = slice(r * ln_rows, (r + 1) * ln_rows)
        y = ALPHA * x_ref[rows, :] + of_ref[rows, :]
        mu = jnp.mean(y, axis=-1, keepdims=True)
        d = y - mu
        var = jnp.mean(d * d, axis=-1, keepdims=True)
        y = d * lax.rsqrt(var + LN_EPS) * g_ref[...] + b_ref[...]
        of_ref[rows, :] = y
        ob_ref[rows, :] = y.astype(BF16)


def _mm_res_ln_kernel(a_ref, w_ref, x_ref, g_ref, b_ref, of_ref, ob_ref, *, ln_rows, n_k):
    if n_k == 1:
        of_ref[...] = _dot(a_ref[...], w_ref[...])
        _residual_layer_norm(x_ref, g_ref, b_ref, of_ref, ob_ref, ln_rows)
        return

    kk = pl.program_id(1)

    @pl.when(kk == 0)
    def _():
        of_ref[...] = _dot(a_ref[...], w_ref[...])

    @pl.when((kk > 0) & (kk < n_k - 1))
    def _():
        of_ref[...] += _dot(a_ref[...], w_ref[...])

    @pl.when(kk == n_k - 1)
    def _():
        of_ref[...] += _dot(a_ref[...], w_ref[...])
        _residual_layer_norm(x_ref, g_ref, b_ref, of_ref, ob_ref, ln_rows)


RESIDENT_WEIGHT_BYTES = 16 * 1024 * 1024


def _mm_res_ln(a_bf, w_bf, layer, x_f32, g, b):
    m, k = a_bf.shape
    if k * D_MODEL * 2 <= RESIDENT_WEIGHT_BYTES:
        bm, bk = min(512, m), k
        w_spec = pl.BlockSpec((None, bk, D_MODEL), lambda i, kk: (layer, 0, 0),
                              pipeline_mode=pl.Buffered(1))
    else:
        bm, bk = min(1024, m), 512
        w_spec = pl.BlockSpec((None, bk, D_MODEL), lambda i, kk: (layer, kk, 0))
    n_k = k // bk
    return pl.pallas_call(
        functools.partial(_mm_res_ln_kernel, ln_rows=min(256, bm), n_k=n_k),
        grid=(m // bm, n_k),
        in_specs=[
            pl.BlockSpec((bm, bk), lambda i, kk: (i, kk)),
            w_spec,
            pl.BlockSpec((bm, D_MODEL), lambda i, kk: (i, 0)),
            pl.BlockSpec((1, D_MODEL), lambda i, kk: (0, 0)),
            pl.BlockSpec((1, D_MODEL), lambda i, kk: (0, 0)),
        ],
        out_specs=[
            pl.BlockSpec((bm, D_MODEL), lambda i, kk: (i, 0)),
            pl.BlockSpec((bm, D_MODEL), lambda i, kk: (i, 0)),
        ],
        out_shape=[
            jax.ShapeDtypeStruct((m, D_MODEL), F32),
            jax.ShapeDtypeStruct((m, D_MODEL), BF16),
        ],
        compiler_params=_params(("parallel", "arbitrary")),
        name="mm_res_ln",
    )(a_bf, w_bf, x_f32, g.reshape(1, D_MODEL), b.reshape(1, D_MODEL))


def _rms(x, g):
    return x * lax.rsqrt(jnp.mean(x * x, axis=-1, keepdims=True) + RMS_EPS) * g


def _mla_in_kernel(x_ref, w_ref, qg_ref, kg_ref, tab_ref, cq_ref, ckv_ref, kr_ref):
    acc = _dot(x_ref[...], w_ref[...])
    cq_ref[...] = _rms(acc[:, :MLA_Q_RANK], qg_ref[...]).astype(BF16)
    ckv_ref[...] = _rms(acc[:, MLA_Q_RANK:MLA_Q_RANK + MLA_KV_RANK], kg_ref[...]).astype(BF16)
    part = acc[:, MLA_Q_RANK + MLA_KV_RANK:] * tab_ref[...]
    kr_ref[...] = (part + pltpu.roll(part, 64, axis=1)).astype(BF16)


def _mla_in_proj(x_bf, w_bf, layer, q_norm, kv_norm, ktab, seq):
    m = x_bf.shape[0]
    bm = min(512, seq)
    n_pos_blocks = seq // bm
    return pl.pallas_call(
        _mla_in_kernel,
        grid=(m // bm,),
        in_specs=[
            pl.BlockSpec((bm, D_MODEL), lambda i: (i, 0)),
            pl.BlockSpec((None, D_MODEL, MLA_IN_W), lambda i: (layer, 0, 0)),
            pl.BlockSpec((1, MLA_Q_RANK), lambda i: (0, 0)),
            pl.BlockSpec((1, MLA_KV_RANK), lambda i: (0, 0)),
            pl.BlockSpec((bm, 128), lambda i: (i % n_pos_blocks, 0)),
        ],
        out_specs=[
            pl.BlockSpec((bm, MLA_Q_RANK), lambda i: (i, 0)),
            pl.BlockSpec((bm, MLA_KV_RANK), lambda i: (i, 0)),
            pl.BlockSpec((bm, 128), lambda i: (i, 0)),
        ],
        out_shape=[
            jax.ShapeDtypeStruct((m, MLA_Q_RANK), BF16),
            jax.ShapeDtypeStruct((m, MLA_KV_RANK), BF16),
            jax.ShapeDtypeStruct((m, 128), BF16),
        ],
        compiler_params=_params(("parallel",)),
        name="mla_in_proj",
    )(x_bf, w_bf, q_norm.reshape(1, -1), kv_norm.reshape(1, -1), ktab)


def _mla_q_kernel(c_ref, w_ref, tab_ref, o_ref, *, heads_per_tile):
    acc = _dot(c_ref[...], w_ref[...])
    tab = tab_ref[...]
    for h in range(heads_per_tile):
        lo = h * MLA_HEAD_W
        o_ref[:, lo:lo + MLA_HEAD_W] = (acc[:, lo:lo + MLA_HEAD_W] * tab).astype(BF16)


def _mla_q_proj(cq, w_bf, layer, qtab, seq):
    m = cq.shape[0]
    bm = min(1024, seq)
    bn = 1024
    n_pos_blocks = seq // bm
    n_out = MLA_HEADS * MLA_HEAD_W
    return pl.pallas_call(
        functools.partial(_mla_q_kernel, heads_per_tile=bn // MLA_HEAD_W),
        grid=(m // bm, n_out // bn),
        in_specs=[
            pl.BlockSpec((bm, MLA_Q_RANK), lambda i, j: (i, 0)),
            pl.BlockSpec((None, MLA_Q_RANK, bn), lambda i, j: (layer, 0, j)),
            pl.BlockSpec((bm, MLA_HEAD_W), lambda i, j: (i % n_pos_blocks, 0)),
        ],
        out_specs=pl.BlockSpec((bm, bn), lambda i, j: (i, j)),
        out_shape=jax.ShapeDtypeStruct((m, n_out), BF16),
        compiler_params=_params(("parallel", "arbitrary")),
        name="mla_q_proj",
    )(cq, w_bf, qtab)


def _mla_kv_kernel(c_ref, w_ref, kr_ref, k_ref, v_ref):
    acc = _dot(c_ref[...], w_ref[...])
    kr = kr_ref[...]
    ones = jnp.ones((acc.shape[0], MLA_V_W - MLA_V), BF16)
    v_first = MLA_HEADS * MLA_NOPE
    for h in range(MLA_HEADS):
        lo = h * MLA_HEAD_W
        k_ref[:, lo:lo + MLA_NOPE] = acc[:, h * MLA_NOPE:(h + 1) * MLA_NOPE].astype(BF16)
        k_ref[:, lo + MLA_NOPE:lo + MLA_HEAD_W] = kr
        lo = h * MLA_V_W
        v_ref[:, lo:lo + MLA_V] = acc[:, v_first + h * MLA_V:v_first + (h + 1) * MLA_V].astype(BF16)
        v_ref[:, lo + MLA_V:lo + MLA_V_W] = ones


def _mla_kv_proj(ckv, w_bf, layer, kr):
    m = ckv.shape[0]
    bm = min(512, m)
    return pl.pallas_call(
        _mla_kv_kernel,
        grid=(m // bm,),
        in_specs=[
            pl.BlockSpec((bm, MLA_KV_RANK), lambda i: (i, 0)),
            pl.BlockSpec((None, MLA_KV_RANK, MLA_HEADS * (MLA_NOPE + MLA_V)), lambda i: (layer, 0, 0)),
            pl.BlockSpec((bm, 128), lambda i: (i, 0)),
        ],
        out_specs=[
            pl.BlockSpec((bm, MLA_HEADS * MLA_HEAD_W), lambda i: (i, 0)),
            pl.BlockSpec((bm, MLA_HEADS * MLA_V_W), lambda i: (i, 0)),
        ],
        out_shape=[
            jax.ShapeDtypeStruct((m, MLA_HEADS * MLA_HEAD_W), BF16),
            jax.ShapeDtypeStruct((m, MLA_HEADS * MLA_V_W), BF16),
        ],
        compiler_params=_params(("parallel",)),
        name="mla_kv_proj",
    )(ckv, w_bf, kr)


def _attn_kernel(q_ref, k_ref, v_ref, o_ref, *, kc, group):
    q = q_ref[0]
    bq = q.shape[0]
    n_groups = k_ref.shape[1] // (kc * group)

    def kv_rows(g, c):
        return pl.ds(pl.multiple_of((g * group + c) * kc, kc), kc)

    def body(g, carry):
        m, acc = carry
        s_next = _dot_nt(q, k_ref[0, kv_rows(g, 0), :])
        for c in range(group):
            s = s_next
            if c + 1 < group:
                s_next = _dot_nt(q, k_ref[0, kv_rows(g, c + 1), :])
            m_new = jnp.maximum(m, jnp.max(s, axis=-1, keepdims=True))
            alpha = jnp.exp2(m - m_new)
            p = jnp.exp2(s - m_new).astype(BF16)
            acc = alpha * acc + _dot(p, v_ref[0, kv_rows(g, c), :])
            m = m_new
        return m, acc

    m0 = jnp.full((bq, 1), -jnp.inf, F32)
    a0 = jnp.zeros((bq, MLA_V_W), F32)
    _, acc = lax.fori_loop(0, n_groups, body, (m0, a0))
    o_ref[0] = (acc[:, :MLA_V] / acc[:, MLA_V:]).astype(BF16)


def _attention(q, k, v, batch, seq):
    bq = min(1024, seq)
    kc = min(512, seq)
    group = min(16, seq // kc)
    q3 = q.reshape(batch, seq, MLA_HEADS * MLA_HEAD_W)
    k3 = k.reshape(batch, seq, MLA_HEADS * MLA_HEAD_W)
    v3 = v.reshape(batch, seq, MLA_HEADS * MLA_V_W)
    return pl.pallas_call(
        functools.partial(_attn_kernel, kc=kc, group=group),
        grid=(batch, MLA_HEADS, seq // bq),
        in_specs=[
            pl.BlockSpec((1, bq, MLA_HEAD_W), lambda b, h, i: (b, i, h)),
            pl.BlockSpec((1, seq, MLA_HEAD_W), lambda b, h, i: (b, 0, h)),
            pl.BlockSpec((1, seq, MLA_V_W), lambda b, h, i: (b, 0, h)),
        ],
        out_specs=pl.BlockSpec((1, bq, MLA_V), lambda b, h, i: (b, i, h)),
        out_shape=jax.ShapeDtypeStruct((batch, seq, MLA_HEADS * MLA_V), BF16),
        compiler_params=_params(("parallel", "parallel", "arbitrary")),
        name="attention",
    )(q3, k3, v3).reshape(batch * seq, MLA_HEADS * MLA_V)


def _ffn_halo_kernel(x_ref, w_ref, o_ref):
    o_ref[...] = _dot(x_ref[...], w_ref[...])


def _ffn_halo(x_rows, w_up_bf, layer):
    r = x_rows.shape[0]
    bn = 512
    g_first = FFN_DIM // bn
    return pl.pallas_call(
        _ffn_halo_kernel,
        grid=(FFN_DIM // bn,),
        in_specs=[
            pl.BlockSpec((r, D_MODEL), lambda j: (0, 0)),
            pl.BlockSpec((None, D_MODEL, bn), lambda j: (layer, 0, g_first + j)),
        ],
        out_specs=pl.BlockSpec((r, bn), lambda j: (0, j)),
        out_shape=jax.ShapeDtypeStruct((r, FFN_DIM), F32),
        compiler_params=_params(("parallel",)),
        name="ffn_halo",
    )(x_rows, w_up_bf)


def _ffn_up_kernel(x_ref, wu_ref, wg_ref, cw_ref, cb_ref, hp_ref, hn_ref, o_ref):
    x = x_ref[...]
    bm = x.shape[0]
    for t in range(o_ref.shape[1] // FFN_GROUP):
        cols = slice(t * FFN_GROUP, (t + 1) * FFN_GROUP)
        acc = _dot(x, jnp.concatenate([wu_ref[:, cols], wg_ref[:, cols]], axis=1))
        u = acc[:, :FFN_GROUP]
        g = acc[:, FFN_GROUP:]
        row = lax.broadcasted_iota(jnp.int32, g.shape, 0)
        g_prev = jnp.where(row == 0, hp_ref[0, :, cols], pltpu.roll(g, 1, axis=0))
        g_next = jnp.where(row == bm - 1, hn_ref[0, :, cols], pltpu.roll(g, bm - 1, axis=0))
        gc = (g_prev * cw_ref[0:1, cols] + g * cw_ref[1:2, cols] + g_next * cw_ref[2:3, cols]
              + cb_ref[:, cols])
        o_ref[:, cols] = (gc * jax.nn.sigmoid(gc) * u).astype(BF16)


def _ffn_up(x_bf, w_up_bf, layer, conv_w, conv_b, seq):
    m = x_bf.shape[0]
    bm = min(1024, seq)
    bn = FFN_DIM // 4
    nmb = m // bm
    g_first = FFN_DIM // bn
    xr = x_bf.reshape(nmb, bm, D_MODEL)
    tile_start = np.arange(nmb) * bm
    has_prev = jnp.asarray((tile_start % seq) != 0)[:, None]
    has_next = jnp.asarray(((tile_start + bm) % seq) != 0)[:, None]
    x_prev = jnp.where(has_prev, jnp.roll(xr[:, bm - 1, :], 1, axis=0), 0).astype(BF16)
    x_next = jnp.where(has_next, jnp.roll(xr[:, 0, :], -1, axis=0), 0).astype(BF16)
    pad = (-2 * nmb) % 16
    rows = jnp.concatenate([x_prev, x_next, jnp.zeros((pad, D_MODEL), BF16)], axis=0)
    halo = _ffn_halo(rows, w_up_bf, layer)
    h_prev = halo[:nmb].reshape(nmb, 1, FFN_DIM)
    h_next = halo[nmb:2 * nmb].reshape(nmb, 1, FFN_DIM)
    return pl.pallas_call(
        _ffn_up_kernel,
        grid=(nmb, FFN_DIM // bn),
        in_specs=[
            pl.BlockSpec((bm, D_MODEL), lambda i, j: (i, 0)),
            pl.BlockSpec((None, D_MODEL, bn), lambda i, j: (layer, 0, j)),
            pl.BlockSpec((None, D_MODEL, bn), lambda i, j: (layer, 0, g_first + j)),
            pl.BlockSpec((3, bn), lambda i, j: (0, j)),
            pl.BlockSpec((1, bn), lambda i, j: (0, j)),
            pl.BlockSpec((1, 1, bn), lambda i, j: (i, 0, j)),
            pl.BlockSpec((1, 1, bn), lambda i, j: (i, 0, j)),
        ],
        out_specs=pl.BlockSpec((bm, bn), lambda i, j: (i, j)),
        out_shape=jax.ShapeDtypeStruct((m, FFN_DIM), BF16),
        compiler_params=_params(("parallel", "arbitrary")),
        name="ffn_up",
    )(x_bf, w_up_bf, w_up_bf, conv_w, conv_b.reshape(1, FFN_DIM), h_prev, h_next)


def _rope_tables(seq, dim, base):
    inv = 1.0 / (base ** (jnp.arange(0, dim // 2, dtype=F32) * (2.0 / dim)))
    ang = jnp.arange(seq, dtype=F32)[:, None] * inv[None, :]
    return jnp.cos(ang), jnp.sin(ang)


def _rotary_quad(r):
    half = MLA_ROPE // 2
    x1, x2 = r[..., :half], r[..., half:]
    return jnp.concatenate([x1, x2, x2, x1], axis=-1)


def _mla_weight_layouts(w_in, w_uq, w_ukv):
    n_layers = w_in.shape[0]
    lat = MLA_Q_RANK + MLA_KV_RANK
    w_in = jnp.concatenate([w_in[..., :lat], _rotary_quad(w_in[..., lat:])], axis=-1)
    q = w_uq.reshape(n_layers, MLA_Q_RANK, MLA_HEADS, MLA_NOPE + MLA_ROPE)
    q = jnp.concatenate([q[..., :MLA_NOPE], _rotary_quad(q[..., MLA_NOPE:])], axis=-1)
    w_uq = q.reshape(n_layers, MLA_Q_RANK, MLA_HEADS * MLA_HEAD_W)
    kv = w_ukv.reshape(n_layers, MLA_KV_RANK, MLA_HEADS, MLA_NOPE + MLA_V)
    w_ukv = jnp.concatenate([kv[..., :MLA_NOPE].reshape(n_layers, MLA_KV_RANK, MLA_HEADS * MLA_NOPE),
                             kv[..., MLA_NOPE:].reshape(n_layers, MLA_KV_RANK, MLA_HEADS * MLA_V)],
                            axis=-1)
    return w_in, w_uq, w_ukv


def _trunk(x, seq, w):
    batch = x.shape[0]
    m = batch * seq
    x_f = x.reshape(m, D_MODEL)
    x_b = x_f.astype(BF16)

    ret_cos, ret_sin = _rope_tables(seq, RET_QK_DIM, RET_ROPE_BASE)
    cos, sin = _rope_tables(seq, MLA_ROPE, MLA_ROPE_BASE)
    rot = jnp.concatenate([cos, cos, -sin, sin], axis=1)
    score_scale = (MLA_NOPE + MLA_ROPE) ** -0.5 * LOG2_E
    qtab = jnp.concatenate([jnp.ones((seq, MLA_NOPE), F32), rot], axis=1) * score_scale

    for i in range(DEPTH):
        j = i // 2
        if i % 2 == 0:
            p = _ret_in_proj(x_b, w["ret_w_in"], j, ret_cos, ret_sin, seq)
            mix = _retention(p, w["ret_decay_fwd"], w["ret_decay_bwd"], j, batch, seq)
            x_f, x_b = _mm_res_ln(mix, w["ret_w_out"], j, x_f, w["ln1_g"][i], w["ln1_b"][i])
        else:
            cq, ckv, kr = _mla_in_proj(x_b, w["mla_w_in"], j, w["mla_q_norm"][j],
                                       w["mla_kv_norm"][j], rot, seq)
            q = _mla_q_proj(cq, w["mla_w_uq"], j, qtab, seq)
            k, v = _mla_kv_proj(ckv, w["mla_w_ukv"], j, kr)
            mix = _attention(q, k, v, batch, seq)
            x_f, x_b = _mm_res_ln(mix, w["mla_w_out"], j, x_f, w["ln1_g"][i], w["ln1_b"][i])
        a = _ffn_up(x_b, w["ffn_w_up"], i, w["ffn_conv_w"][i], w["ffn_conv_b"][i], seq)
        x_f, x_b = _mm_res_ln(a, w["ffn_w_down"], i, x_f, w["ln2_g"][i], w["ln2_b"][i])
    return x_f.reshape(batch, seq, D_MODEL)


def _prepare_weights(ret_w_in, ret_decay_fwd, ret_decay_bwd, ret_w_out,
                     mla_w_in, mla_q_norm, mla_kv_norm, mla_w_uq, mla_w_ukv, mla_w_out,
                     ln1_g, ln1_b, ln2_g, ln2_b, ffn_w_up, ffn_conv_w, ffn_conv_b, ffn_w_down):
    mla_w_in, mla_w_uq, mla_w_ukv = _mla_weight_layouts(mla_w_in, mla_w_uq, mla_w_ukv)

    def lanes(d):
        return jnp.broadcast_to(d.astype(F32)[:, :, None, None], d.shape + (1, 128))

    return {
        "ret_w_in": ret_w_in.astype(BF16),
        "ret_decay_fwd": lanes(ret_decay_fwd),
        "ret_decay_bwd": lanes(ret_decay_bwd),
        "ret_w_out": ret_w_out.astype(BF16),
        "mla_w_in": mla_w_in.astype(BF16),
        "mla_q_norm": mla_q_norm.astype(F32),
        "mla_kv_norm": mla_kv_norm.astype(F32),
        "mla_w_uq": mla_w_uq.astype(BF16),
        "mla_w_ukv": mla_w_ukv.astype(BF16),
        "mla_w_out": mla_w_out.astype(BF16),
        "ln1_g": ln1_g.astype(F32), "ln1_b": ln1_b.astype(F32),
        "ln2_g": ln2_g.astype(F32), "ln2_b": ln2_b.astype(F32),
        "ffn_w_up": ffn_w_up.astype(BF16),
        "ffn_conv_w": ffn_conv_w.astype(F32),
        "ffn_conv_b": ffn_conv_b.astype(F32),
        "ffn_w_down": ffn_w_down.astype(BF16),
    }


def kernel(x_prompt, x_sample, ret_w_in, ret_decay_fwd, ret_decay_bwd, ret_w_out, mla_w_in, mla_q_norm, mla_kv_norm, mla_w_uq, mla_w_ukv, mla_w_out, ln1_g, ln1_b, ln2_g, ln2_b, ffn_w_up, ffn_conv_w, ffn_conv_b, ffn_w_down):
    w = _prepare_weights(ret_w_in, ret_decay_fwd, ret_decay_bwd, ret_w_out,
                         mla_w_in, mla_q_norm, mla_kv_norm, mla_w_uq, mla_w_ukv, mla_w_out,
                         ln1_g, ln1_b, ln2_g, ln2_b, ffn_w_up, ffn_conv_w, ffn_conv_b, ffn_w_down)
    y_prompt = _trunk(x_prompt, x_prompt.shape[1], w)
    y_sample = _trunk(x_sample, x_sample.shape[1], w)
    return (y_prompt, y_sample)
```

```python
import functools
import math

import jax
import jax.numpy as jnp
import numpy as np
from jax import lax
from jax.experimental import pallas as pl
from jax.experimental.pallas import tpu as pltpu

F32 = jnp.float32
BF16 = jnp.bfloat16

D_MODEL = 2048
DEPTH = 4
ALPHA = (2.0 * DEPTH) ** 0.25

RET_HEADS = 8
RET_QK_DIM = 256
RET_V_DIM = 512
RET_QK_W = RET_HEADS * RET_QK_DIM
RET_V_W = RET_HEADS * RET_V_DIM
RET_IN = 2 * RET_QK_W + 2 * RET_V_W
RET_ROPE_BASE = 10000.0

MLA_HEADS = 16
MLA_Q_RANK = 1536
MLA_KV_RANK = 512
MLA_NOPE = 128
MLA_ROPE = 64
MLA_V = 128
MLA_ROPE_BASE = 10000.0
MLA_HEAD_W = 256
MLA_IN_W = MLA_Q_RANK + MLA_KV_RANK + 128
MLA_V_W = 256

FFN_DIM = 5632
FFN_GROUP = 128

LN_EPS = 1e-5
RMS_EPS = 1e-6
GN_EPS = 1e-6

VMEM_LIMIT_BYTES = 56 * 1024 * 1024
LOG2_E = 1.4426950408889634


def _params(semantics):
    return pltpu.CompilerParams(dimension_semantics=semantics,
                                vmem_limit_bytes=VMEM_LIMIT_BYTES)


def _dot(a, b):
    return jnp.dot(a, b, preferred_element_type=F32)


def _dot_nt(a, b):
    return lax.dot_general(a, b, (((1,), (1,)), ((), ())), preferred_element_type=F32)


def _dot_tn(a, b):
    return lax.dot_general(a, b, (((0,), (0,)), ((), ())), preferred_element_type=F32)


def _ret_in_kernel(x_ref, w_ref, cos_ref, sin_ref, o_ref, *, bn):
    j = pl.program_id(1)
    n_qk_tiles = 2 * RET_QK_W // bn

    @pl.when(j < n_qk_tiles)
    def _():
        cos = cos_ref[...]
        sin = sin_ref[...]
        scale = jnp.where(j >= n_qk_tiles // 2, RET_QK_DIM ** -0.5, 1.0).astype(F32)
        half = RET_QK_DIM // 2
        for h in range(bn // RET_QK_DIM):
            lo = h * RET_QK_DIM
            acc = _dot(x_ref[...], w_ref[:, lo:lo + RET_QK_DIM])
            x1 = acc[:, :half]
            x2 = acc[:, half:]
            o_ref[:, lo:lo + half] = ((x1 * cos - x2 * sin) * scale).astype(BF16)
            o_ref[:, lo + half:lo + 2 * half] = ((x1 * sin + x2 * cos) * scale).astype(BF16)

    n_v_tiles = RET_V_W // bn

    @pl.when((j >= n_qk_tiles) & (j < n_qk_tiles + n_v_tiles))
    def _():
        for h in range(bn // RET_QK_DIM):
            cols = slice(h * RET_QK_DIM, (h + 1) * RET_QK_DIM)
            o_ref[:, cols] = _dot(x_ref[...], w_ref[:, cols]).astype(BF16)

    @pl.when(j >= n_qk_tiles + n_v_tiles)
    def _():
        for h in range(bn // RET_QK_DIM):
            cols = slice(h * RET_QK_DIM, (h + 1) * RET_QK_DIM)
            gate = _dot(x_ref[...], w_ref[:, cols])
            o_ref[:, cols] = (gate * jax.nn.sigmoid(gate)).astype(BF16)


def _ret_in_proj(x_bf, w_bf, layer, cos, sin, seq):
    m = x_bf.shape[0]
    bm = min(2048, seq)
    bn = 1024
    n_pos_blocks = seq // bm
    return pl.pallas_call(
        functools.partial(_ret_in_kernel, bn=bn),
        grid=(m // bm, RET_IN // bn),
        in_specs=[
            pl.BlockSpec((bm, D_MODEL), lambda i, j: (i, 0)),
            pl.BlockSpec((None, D_MODEL, bn), lambda i, j: (layer, 0, j)),
            pl.BlockSpec((bm, RET_QK_DIM // 2), lambda i, j: (i % n_pos_blocks, 0)),
            pl.BlockSpec((bm, RET_QK_DIM // 2), lambda i, j: (i % n_pos_blocks, 0)),
        ],
        out_specs=pl.BlockSpec((bm, bn), lambda i, j: (i, j)),
        out_shape=jax.ShapeDtypeStruct((m, RET_IN), BF16),
        compiler_params=_params(("parallel", "arbitrary")),
        name="ret_in_proj",
    )(x_bf, w_bf, cos, sin)


def _log_sigmoid(x):
    return jnp.minimum(x, 0.0) - jnp.log1p(jnp.exp(-jnp.abs(x)))


def _retention_kernel(q_ref, k_ref, v_ref, g_ref, df_ref, db_ref, o_ref,
                      state_ref, obuf_ref, dmat_ref, xi_ref, zeta_ref, *, chunk, n_sub):
    ps = pl.program_id(2)
    t = pl.program_id(3)
    nt = pl.num_programs(3)
    c = chunk

    @pl.when(t == 0)
    def _():
        state_ref[...] = jnp.zeros_like(state_ref)

    @pl.when((t == 0) & (ps == 0))
    def _():
        lf = _log_sigmoid(df_ref[...])
        lb = _log_sigmoid(db_ref[...])
        row = lax.broadcasted_iota(jnp.int32, (c, 128), 0).astype(F32)
        xi_ref[0] = jnp.exp(lb * (c - row))
        xi_ref[1] = jnp.exp(lf * (row + 1.0))
        zeta_ref[0] = jnp.exp(lb * row)
        zeta_ref[1] = jnp.exp(lf * (c - 1.0 - row))
        for jb in range(c // 128):
            col = lax.broadcasted_iota(jnp.int32, (c, 128), 1).astype(F32) + (128.0 * jb)
            rel = row - col
            dmat_ref[:, jb * 128:(jb + 1) * 128] = jnp.where(
                rel >= 0.0, jnp.exp(lf * jnp.maximum(rel, 0.0)), jnp.exp(lb * jnp.maximum(-rel, 0.0)))

    def tile128(tab, width):
        return jnp.concatenate([tab] * (width // 128), axis=1)

    def state_step(qc, kc, vc, direction):
        xi = xi_ref[direction]
        zeta = zeta_ref[direction]
        st = state_ref[...]
        inter = _dot(qc, st.astype(BF16)) * tile128(xi, RET_V_DIM)
        kz = (kc.astype(F32) * tile128(zeta, RET_QK_DIM)).astype(BF16)
        g_chunk = xi[c - 1:c, :] if direction == 1 else xi[0:1, :]
        state_ref[...] = st * tile128(g_chunk, RET_V_DIM) + _dot_tn(kz, vc)
        return inter

    @pl.when(ps == 0)
    def _():
        for s in reversed(range(n_sub)):
            rows = pl.ds(s * c, c)
            inter = state_step(q_ref[0, rows, :], k_ref[0, rows, :], v_ref[0, rows, :], 0)
            base = pl.multiple_of(((nt - 1 - t) * n_sub + s) * c, c)
            obuf_ref[pl.ds(base, c), :] = inter

    @pl.when(ps == 1)
    def _():
        for s in range(n_sub):
            rows = pl.ds(s * c, c)
            qc = q_ref[0, rows, :]
            kc = k_ref[0, rows, :]
            vc = v_ref[0, rows, :]
            sc = (_dot_nt(qc, kc) * dmat_ref[...]).astype(BF16)
            base = pl.multiple_of((t * n_sub + s) * c, c)
            o = _dot(sc, vc) + obuf_ref[pl.ds(base, c), :]
            o = o + state_step(qc, kc, vc, 1)
            mu = jnp.mean(o, axis=-1, keepdims=True)
            d = o - mu
            var = jnp.mean(d * d, axis=-1, keepdims=True)
            o = d * lax.rsqrt(var + GN_EPS)
            o_ref[0, rows, :] = (g_ref[0, rows, :].astype(F32) * o).astype(BF16)


def _retention(p, decay_f, decay_b, layer, batch, seq):
    chunk = min(256, seq)
    tb = min(2048, seq)
    n_sub = tb // chunk
    nt = seq // tb
    p3 = p.reshape(batch, seq, RET_IN)
    qk_blocks = RET_QK_W // RET_QK_DIM
    v_first = 2 * RET_QK_W // RET_V_DIM
    g_first = v_first + RET_HEADS

    def pos(ps, t):
        return jnp.where(ps == 0, nt - 1 - t, t)

    return pl.pallas_call(
        functools.partial(_retention_kernel, chunk=chunk, n_sub=n_sub),
        grid=(batch, RET_HEADS, 2, nt),
        in_specs=[
            pl.BlockSpec((1, tb, RET_QK_DIM), lambda b, h, ps, t: (b, pos(ps, t), h)),
            pl.BlockSpec((1, tb, RET_QK_DIM), lambda b, h, ps, t: (b, pos(ps, t), qk_blocks + h)),
            pl.BlockSpec((1, tb, RET_V_DIM), lambda b, h, ps, t: (b, pos(ps, t), v_first + h)),
            pl.BlockSpec((1, tb, RET_V_DIM), lambda b, h, ps, t: (b, t * ps, g_first + h)),
            pl.BlockSpec((None, None, 1, 128), lambda b, h, ps, t: (layer, h, 0, 0)),
            pl.BlockSpec((None, None, 1, 128), lambda b, h, ps, t: (layer, h, 0, 0)),
        ],
        out_specs=pl.BlockSpec((1, tb, RET_V_DIM), lambda b, h, ps, t: (b, t * ps, h)),
        out_shape=jax.ShapeDtypeStruct((batch, seq, RET_V_W), BF16),
        scratch_shapes=[
            pltpu.VMEM((RET_QK_DIM, RET_V_DIM), F32),
            pltpu.VMEM((seq, RET_V_DIM), F32),
            pltpu.VMEM((chunk, chunk), F32),
            pltpu.VMEM((2, chunk, 128), F32),
            pltpu.VMEM((2, chunk, 128), F32),
        ],
        compiler_params=_params(("parallel", "parallel", "arbitrary", "arbitrary")),
        name="retention",
    )(p3, p3, p3, p3, decay_f, decay_b).reshape(batch * seq, RET_V_W)


def _residual_layer_norm(x_ref, g_ref, b_ref, of_ref, ob_ref, ln_rows):
    for r in range(of_ref.shape[0] // ln_rows):
        rows = slice(r * ln_rows, (r + 1) * ln_rows)
        y = ALPHA * x_ref[rows, :] + of_ref[rows, :]
        mu = jnp.mean(y, axis=-1, keepdims=True)
        d = y - mu
        var = jnp.mean(d * d, axis=-1, keepdims=True)
        y = d * lax.rsqrt(var + LN_EPS) * g_ref[...] + b_ref[...]
        of_ref[rows, :] = y
        ob_ref[rows, :] = y.astype(BF16)


def _mm_res_ln_kernel(a_ref, w_ref, x_ref, g_ref, b_ref, of_ref, ob_ref, *, ln_rows, n_k):
    if n_k == 1:
        of_ref[...] = _dot(a_ref[...], w_ref[...])
        _residual_layer_norm(x_ref, g_ref, b_ref, of_ref, ob_ref, ln_rows)
        return

    kk = pl.program_id(1)

    @pl.when(kk == 0)
    def _():
        of_ref[...] = _dot(a_ref[...], w_ref[...])

    @pl.when((kk > 0) & (kk < n_k - 1))
    def _():
        of_ref[...] += _dot(a_ref[...], w_ref[...])

    @pl.when(kk == n_k - 1)
    def _():
        of_ref[...] += _dot(a_ref[...], w_ref[...])
        _residual_layer_norm(x_ref, g_ref, b_ref, of_ref, ob_ref, ln_rows)


RESIDENT_WEIGHT_BYTES = 16 * 1024 * 1024


def _mm_res_ln(a_bf, w_bf, layer, x_f32, g, b):
    m, k = a_bf.shape
    if k * D_MODEL * 2 <= RESIDENT_WEIGHT_BYTES:
        bm, bk = min(512, m), k
        w_spec = pl.BlockSpec((None, bk, D_MODEL), lambda i, kk: (layer, 0, 0),
                              pipeline_mode=pl.Buffered(1))
    else:
        bm, bk = min(1024, m), 512
        w_spec = pl.BlockSpec((None, bk, D_MODEL), lambda i, kk: (layer, kk, 0))
    n_k = k // bk
    return pl.pallas_call(
        functools.partial(_mm_res_ln_kernel, ln_rows=min(256, bm), n_k=n_k),
        grid=(m // bm, n_k),
        in_specs=[
            pl.BlockSpec((bm, bk), lambda i, kk: (i, kk)),
            w_spec,
            pl.BlockSpec((bm, D_MODEL), lambda i, kk: (i, 0)),
            pl.BlockSpec((1, D_MODEL), lambda i, kk: (0, 0)),
            pl.BlockSpec((1, D_MODEL), lambda i, kk: (0, 0)),
        ],
        out_specs=[
            pl.BlockSpec((bm, D_MODEL), lambda i, kk: (i, 0)),
            pl.BlockSpec((bm, D_MODEL), lambda i, kk: (i, 0)),
        ],
        out_shape=[
            jax.ShapeDtypeStruct((m, D_MODEL), F32),
            jax.ShapeDtypeStruct((m, D_MODEL), BF16),
        ],
        compiler_params=_params(("parallel", "arbitrary")),
        name="mm_res_ln",
    )(a_bf, w_bf, x_f32, g.reshape(1, D_MODEL), b.reshape(1, D_MODEL))


def _rms(x, g):
    return x * lax.rsqrt(jnp.mean(x * x, axis=-1, keepdims=True) + RMS_EPS) * g


def _mla_in_kernel(x_ref, w_ref, qg_ref, kg_ref, tab_ref, cq_ref, ckv_ref, kr_ref):
    acc = _dot(x_ref[...], w_ref[...])
    cq_ref[...] = _rms(acc[:, :MLA_Q_RANK], qg_ref[...]).astype(BF16)
    ckv_ref[...] = _rms(acc[:, MLA_Q_RANK:MLA_Q_RANK + MLA_KV_RANK], kg_ref[...]).astype(BF16)
    part = acc[:, MLA_Q_RANK + MLA_KV_RANK:] * tab_ref[...]
    kr_ref[...] = (part + pltpu.roll(part, 64, axis=1)).astype(BF16)


def _mla_in_proj(x_bf, w_bf, layer, q_norm, kv_norm, ktab, seq):
    m = x_bf.shape[0]
    bm = min(512, seq)
    n_pos_blocks = seq // bm
    return pl.pallas_call(
        _mla_in_kernel,
        grid=(m // bm,),
        in_specs=[
            pl.BlockSpec((bm, D_MODEL), lambda i: (i, 0)),
            pl.BlockSpec((None, D_MODEL, MLA_IN_W), lambda i: (layer, 0, 0)),
            pl.BlockSpec((1, MLA_Q_RANK), lambda i: (0, 0)),
            pl.BlockSpec((1, MLA_KV_RANK), lambda i: (0, 0)),
            pl.BlockSpec((bm, 128), lambda i: (i % n_pos_blocks, 0)),
        ],
        out_specs=[
            pl.BlockSpec((bm, MLA_Q_RANK), lambda i: (i, 0)),
            pl.BlockSpec((bm, MLA_KV_RANK), lambda i: (i, 0)),
            pl.BlockSpec((bm, 128), lambda i: (i, 0)),
        ],
        out_shape=[
            jax.ShapeDtypeStruct((m, MLA_Q_RANK), BF16),
            jax.ShapeDtypeStruct((m, MLA_KV_RANK), BF16),
            jax.ShapeDtypeStruct((m, 128), BF16),
        ],
        compiler_params=_params(("parallel",)),
        name="mla_in_proj",
    )(x_bf, w_bf, q_norm.reshape(1, -1), kv_norm.reshape(1, -1), ktab)


def _mla_q_kernel(c_ref, w_ref, tab_ref, o_ref, *, heads_per_tile):
    acc = _dot(c_ref[...], w_ref[...])
    tab = tab_ref[...]
    for h in range(heads_per_tile):
        lo = h * MLA_HEAD_W
        o_ref[:, lo:lo + MLA_HEAD_W] = (acc[:, lo:lo + MLA_HEAD_W] * tab).astype(BF16)


def _mla_q_proj(cq, w_bf, layer, qtab, seq):
    m = cq.shape[0]
    bm = min(1024, seq)
    bn = 1024
    n_pos_blocks = seq // bm
    n_out = MLA_HEADS * MLA_HEAD_W
    return pl.pallas_call(
        functools.partial(_mla_q_kernel, heads_per_tile=bn // MLA_HEAD_W),
        grid=(m // bm, n_out // bn),
        in_specs=[
            pl.BlockSpec((bm, MLA_Q_RANK), lambda i, j: (i, 0)),
            pl.BlockSpec((None, MLA_Q_RANK, bn), lambda i, j: (layer, 0, j)),
            pl.BlockSpec((bm, MLA_HEAD_W), lambda i, j: (i % n_pos_blocks, 0)),
        ],
        out_specs=pl.BlockSpec((bm, bn), lambda i, j: (i, j)),
        out_shape=jax.ShapeDtypeStruct((m, n_out), BF16),
        compiler_params=_params(("parallel", "arbitrary")),
        name="mla_q_proj",
    )(cq, w_bf, qtab)


def _mla_kv_kernel(c_ref, w_ref, k_ref, v_ref):
    acc = _dot(c_ref[...], w_ref[...])
    v_first = MLA_HEADS * MLA_NOPE
    k_ref[...] = acc[:, :v_first].astype(BF16)
    v_ref[...] = acc[:, v_first:].astype(BF16)


def _mla_kv_proj(ckv, w_bf, layer):
    m = ckv.shape[0]
    bm = min(512, m)
    return pl.pallas_call(
        _mla_kv_kernel,
        grid=(m // bm,),
        in_specs=[
            pl.BlockSpec((bm, MLA_KV_RANK), lambda i: (i, 0)),
            pl.BlockSpec((None, MLA_KV_RANK, MLA_HEADS * (MLA_NOPE + MLA_V)), lambda i: (layer, 0, 0)),
        ],
        out_specs=[
            pl.BlockSpec((bm, MLA_HEADS * MLA_NOPE), lambda i: (i, 0)),
            pl.BlockSpec((bm, MLA_HEADS * MLA_V), lambda i: (i, 0)),
        ],
        out_shape=[
            jax.ShapeDtypeStruct((m, MLA_HEADS * MLA_NOPE), BF16),
            jax.ShapeDtypeStruct((m, MLA_HEADS * MLA_V), BF16),
        ],
        compiler_params=_params(("parallel",)),
        name="mla_kv_proj",
    )(ckv, w_bf)


def _attn_kernel(q_ref, kn_ref, kr_ref, v_ref, o_ref, k_buf, v_buf, *, kc, group):
    @pl.when(pl.program_id(2) == 0)
    def _():
        k_buf[:, :MLA_NOPE] = kn_ref[0]
        k_buf[:, MLA_NOPE:] = kr_ref[0]
        v_buf[:, :MLA_V] = v_ref[0]
        v_buf[:, MLA_V:] = jnp.ones((v_buf.shape[0], MLA_V_W - MLA_V), BF16)

    q = q_ref[0]
    bq = q.shape[0]
    n_groups = k_buf.shape[0] // (kc * group)

    def kv_rows(g, c):
        return pl.ds(pl.multiple_of((g * group + c) * kc, kc), kc)

    def body(g, carry):
        m, acc = carry
        s_next = _dot_nt(q, k_buf[kv_rows(g, 0), :])
        for c in range(group):
            s = s_next
            if c + 1 < group:
                s_next = _dot_nt(q, k_buf[kv_rows(g, c + 1), :])
            m_new = jnp.maximum(m, jnp.max(s, axis=-1, keepdims=True))
            alpha = jnp.exp2(m - m_new)
            p = jnp.exp2(s - m_new).astype(BF16)
            acc = alpha * acc + _dot(p, v_buf[kv_rows(g, c), :])
            m = m_new
        return m, acc

    m0 = jnp.full((bq, 1), -jnp.inf, F32)
    a0 = jnp.zeros((bq, MLA_V_W), F32)
    _, acc = lax.fori_loop(0, n_groups, body, (m0, a0))
    o_ref[0] = (acc[:, :MLA_V] / acc[:, MLA_V:]).astype(BF16)


def _attention(q, k_nope, k_rot, v, batch, seq):
    bq = min(1024, seq)
    kc = min(512, seq)
    group = min(16, seq // kc)
    q3 = q.reshape(batch, seq, MLA_HEADS * MLA_HEAD_W)
    kn3 = k_nope.reshape(batch, seq, MLA_HEADS * MLA_NOPE)
    kr3 = k_rot.reshape(batch, seq, MLA_HEAD_W - MLA_NOPE)
    v3 = v.reshape(batch, seq, MLA_HEADS * MLA_V)
    return pl.pallas_call(
        functools.partial(_attn_kernel, kc=kc, group=group),
        grid=(batch, MLA_HEADS, seq // bq),
        in_specs=[
            pl.BlockSpec((1, bq, MLA_HEAD_W), lambda b, h, i: (b, i, h)),
            pl.BlockSpec((1, seq, MLA_NOPE), lambda b, h, i: (b, 0, h)),
            pl.BlockSpec((1, seq, MLA_HEAD_W - MLA_NOPE), lambda b, h, i: (b, 0, 0)),
            pl.BlockSpec((1, seq, MLA_V), lambda b, h, i: (b, 0, h)),
        ],
        out_specs=pl.BlockSpec((1, bq, MLA_V), lambda b, h, i: (b, i, h)),
        out_shape=jax.ShapeDtypeStruct((batch, seq, MLA_HEADS * MLA_V), BF16),
        scratch_shapes=[pltpu.VMEM((seq, MLA_HEAD_W), BF16), pltpu.VMEM((seq, MLA_V_W), BF16)],
        compiler_params=_params(("parallel", "parallel", "arbitrary")),
        name="attention",
    )(q3, kn3, kr3, v3).reshape(batch * seq, MLA_HEADS * MLA_V)


def _ffn_halo_kernel(x_ref, w_ref, o_ref):
    o_ref[...] = _dot(x_ref[...], w_ref[...])


def _ffn_halo(x_rows, w_up_bf, layer):
    r = x_rows.shape[0]
    bn = 512
    g_first = FFN_DIM // bn
    return pl.pallas_call(
        _ffn_halo_kernel,
        grid=(FFN_DIM // bn,),
        in_specs=[
            pl.BlockSpec((r, D_MODEL), lambda j: (0, 0)),
            pl.BlockSpec((None, D_MODEL, bn), lambda j: (layer, 0, g_first + j)),
        ],
        out_specs=pl.BlockSpec((r, bn), lambda j: (0, j)),
        out_shape=jax.ShapeDtypeStruct((r, FFN_DIM), F32),
        compiler_params=_params(("parallel",)),
        name="ffn_halo",
    )(x_rows, w_up_bf)


def _ffn_up_kernel(x_ref, wu_ref, wg_ref, cw_ref, cb_ref, hp_ref, hn_ref, o_ref):
    x = x_ref[...]
    bm = x.shape[0]
    for t in range(o_ref.shape[1] // FFN_GROUP):
        cols = slice(t * FFN_GROUP, (t + 1) * FFN_GROUP)
        acc = _dot(x, jnp.concatenate([wu_ref[:, cols], wg_ref[:, cols]], axis=1))
        u = acc[:, :FFN_GROUP]
        g = acc[:, FFN_GROUP:]
        row = lax.broadcasted_iota(jnp.int32, g.shape, 0)
        g_prev = jnp.where(row == 0, hp_ref[0, :, cols], pltpu.roll(g, 1, axis=0))
        g_next = jnp.where(row == bm - 1, hn_ref[0, :, cols], pltpu.roll(g, bm - 1, axis=0))
        gc = (g_prev * cw_ref[0:1, cols] + g * cw_ref[1:2, cols] + g_next * cw_ref[2:3, cols]
              + cb_ref[:, cols])
        o_ref[:, cols] = (gc * jax.nn.sigmoid(gc) * u).astype(BF16)


def _ffn_up(x_bf, w_up_bf, layer, conv_w, conv_b, seq):
    m = x_bf.shape[0]
    bm = min(1024, seq)
    bn = FFN_DIM // 4
    nmb = m // bm
    g_first = FFN_DIM // bn
    xr = x_bf.reshape(nmb, bm, D_MODEL)
    tile_start = np.arange(nmb) * bm
    has_prev = jnp.asarray((tile_start % seq) != 0)[:, None]
    has_next = jnp.asarray(((tile_start + bm) % seq) != 0)[:, None]
    x_prev = jnp.where(has_prev, jnp.roll(xr[:, bm - 1, :], 1, axis=0), 0).astype(BF16)
    x_next = jnp.where(has_next, jnp.roll(xr[:, 0, :], -1, axis=0), 0).astype(BF16)
    pad = (-2 * nmb) % 16
    rows = jnp.concatenate([x_prev, x_next, jnp.zeros((pad, D_MODEL), BF16)], axis=0)
    halo = _ffn_halo(rows, w_up_bf, layer)
    h_prev = halo[:nmb].reshape(nmb, 1, FFN_DIM)
    h_next = halo[nmb:2 * nmb].reshape(nmb, 1, FFN_DIM)
    return pl.pallas_call(
        _ffn_up_kernel,
        grid=(nmb, FFN_DIM // bn),
        in_specs=[
            pl.BlockSpec((bm, D_MODEL), lambda i, j: (i, 0)),
            pl.BlockSpec((None, D_MODEL, bn), lambda i, j: (layer, 0, j)),
            pl.BlockSpec((None, D_MODEL, bn), lambda i, j: (layer, 0, g_first + j)),
            pl.BlockSpec((3, bn), lambda i, j: (0, j)),
            pl.BlockSpec((1, bn), lambda i, j: (0, j)),
            pl.BlockSpec((1, 1, bn), lambda i, j: (i, 0, j)),
            pl.BlockSpec((1, 1, bn), lambda i, j: (i, 0, j)),
        ],
        out_specs=pl.BlockSpec((bm, bn), lambda i, j: (i, j)),
        out_shape=jax.ShapeDtypeStruct((m, FFN_DIM), BF16),
        compiler_params=_params(("parallel", "arbitrary")),
        name="ffn_up",
    )(x_bf, w_up_bf, w_up_bf, conv_w, conv_b.reshape(1, FFN_DIM), h_prev, h_next)


def _rope_tables(seq, dim, base):
    inv = 1.0 / (base ** (jnp.arange(0, dim // 2, dtype=F32) * (2.0 / dim)))
    ang = jnp.arange(seq, dtype=F32)[:, None] * inv[None, :]
    return jnp.cos(ang), jnp.sin(ang)


def _rotary_quad(r):
    half = MLA_ROPE // 2
    x1, x2 = r[..., :half], r[..., half:]
    return jnp.concatenate([x1, x2, x2, x1], axis=-1)


def _mla_weight_layouts(w_in, w_uq, w_ukv):
    n_layers = w_in.shape[0]
    lat = MLA_Q_RANK + MLA_KV_RANK
    w_in = jnp.concatenate([w_in[..., :lat], _rotary_quad(w_in[..., lat:])], axis=-1)
    q = w_uq.reshape(n_layers, MLA_Q_RANK, MLA_HEADS, MLA_NOPE + MLA_ROPE)
    q = jnp.concatenate([q[..., :MLA_NOPE], _rotary_quad(q[..., MLA_NOPE:])], axis=-1)
    w_uq = q.reshape(n_layers, MLA_Q_RANK, MLA_HEADS * MLA_HEAD_W)
    kv = w_ukv.reshape(n_layers, MLA_KV_RANK, MLA_HEADS, MLA_NOPE + MLA_V)
    w_ukv = jnp.concatenate([kv[..., :MLA_NOPE].reshape(n_layers, MLA_KV_RANK, MLA_HEADS * MLA_NOPE),
                             kv[..., MLA_NOPE:].reshape(n_layers, MLA_KV_RANK, MLA_HEADS * MLA_V)],
                            axis=-1)
    return w_in, w_uq, w_ukv


def _trunk(x, seq, w):
    batch = x.shape[0]
    m = batch * seq
    x_f = x.reshape(m, D_MODEL)
    x_b = x_f.astype(BF16)

    ret_cos, ret_sin = _rope_tables(seq, RET_QK_DIM, RET_ROPE_BASE)
    cos, sin = _rope_tables(seq, MLA_ROPE, MLA_ROPE_BASE)
    rot = jnp.concatenate([cos, cos, -sin, sin], axis=1)
    score_scale = (MLA_NOPE + MLA_ROPE) ** -0.5 * LOG2_E
    qtab = jnp.concatenate([jnp.ones((seq, MLA_NOPE), F32), rot], axis=1) * score_scale

    for i in range(DEPTH):
        j = i // 2
        if i % 2 == 0:
            p = _ret_in_proj(x_b, w["ret_w_in"], j, ret_cos, ret_sin, seq)
            mix = _retention(p, w["ret_decay_fwd"], w["ret_decay_bwd"], j, batch, seq)
            x_f, x_b = _mm_res_ln(mix, w["ret_w_out"], j, x_f, w["ln1_g"][i], w["ln1_b"][i])
        else:
            cq, ckv, kr = _mla_in_proj(x_b, w["mla_w_in"], j, w["mla_q_norm"][j],
                                       w["mla_kv_norm"][j], rot, seq)
            q = _mla_q_proj(cq, w["mla_w_uq"], j, qtab, seq)
            k_nope, v = _mla_kv_proj(ckv, w["mla_w_ukv"], j)
            mix = _attention(q, k_nope, kr, v, batch, seq)
            x_f, x_b = _mm_res_ln(mix, w["mla_w_out"], j, x_f, w["ln1_g"][i], w["ln1_b"][i])
        a = _ffn_up(x_b, w["ffn_w_up"], i, w["ffn_conv_w"][i], w["ffn_conv_b"][i], seq)
        x_f, x_b = _mm_res_ln(a, w["ffn_w_down"], i, x_f, w["ln2_g"][i], w["ln2_b"][i])
    return x_f.reshape(batch, seq, D_MODEL)


def _prepare_weights(ret_w_in, ret_decay_fwd, ret_decay_bwd, ret_w_out,
                     mla_w_in, mla_q_norm, mla_kv_norm, mla_w_uq, mla_w_ukv, mla_w_out,
                     ln1_g, ln1_b, ln2_g, ln2_b, ffn_w_up, ffn_conv_w, ffn_conv_b, ffn_w_down):
    mla_w_in, mla_w_uq, mla_w_ukv = _mla_weight_layouts(mla_w_in, mla_w_uq, mla_w_ukv)

    def lanes(d):
        return jnp.broadcast_to(d.astype(F32)[:, :, None, None], d.shape + (1, 128))

    return {
        "ret_w_in": ret_w_in.astype(BF16),
        "ret_decay_fwd": lanes(ret_decay_fwd),
        "ret_decay_bwd": lanes(ret_decay_bwd),
        "ret_w_out": ret_w_out.astype(BF16),
        "mla_w_in": mla_w_in.astype(BF16),
        "mla_q_norm": mla_q_norm.astype(F32),
        "mla_kv_norm": mla_kv_norm.astype(F32),
        "mla_w_uq": mla_w_uq.astype(BF16),
        "mla_w_ukv": mla_w_ukv.astype(BF16),
        "mla_w_out": mla_w_out.astype(BF16),
        "ln1_g": ln1_g.astype(F32), "ln1_b": ln1_b.astype(F32),
        "ln2_g": ln2_g.astype(F32), "ln2_b": ln2_b.astype(F32),
        "ffn_w_up": ffn_w_up.astype(BF16),
        "ffn_conv_w": ffn_conv_w.astype(F32),
        "ffn_conv_b": ffn_conv_b.astype(F32),
        "ffn_w_down": ffn_w_down.astype(BF16),
    }


def kernel(x_prompt, x_sample, ret_w_in, ret_decay_fwd, ret_decay_bwd, ret_w_out, mla_w_in, mla_q_norm, mla_kv_norm, mla_w_uq, mla_w_ukv, mla_w_out, ln1_g, ln1_b, ln2_g, ln2_b, ffn_w_up, ffn_conv_w, ffn_conv_b, ffn_w_down):
    w = _prepare_weights(ret_w_in, ret_decay_fwd, ret_decay_bwd, ret_w_out,
                         mla_w_in, mla_q_norm, mla_kv_norm, mla_w_uq, mla_w_ukv, mla_w_out,
                         ln1_g, ln1_b, ln2_g, ln2_b, ffn_w_up, ffn_conv_w, ffn_conv_b, ffn_w_down)
    y_prompt = _trunk(x_prompt, x_prompt.shape[1], w)
    y_sample = _trunk(x_sample, x_sample.shape[1], w)
    return (y_prompt, y_sample)
```
